```python
import jax
import jax.numpy as jnp
from jax import lax
import numpy as np

D_MODEL = 1024
BATCH = 8
SEQ = 2048
DEPTH = 2

GRID_W = 64
CTX_LEN = 256
HEAD_DIM = 64
NA_HEADS = D_MODEL // (2 * HEAD_DIM)
GQA_HEADS = D_MODEL // (2 * HEAD_DIM)
GQA_KV_HEADS = GQA_HEADS // 4
GQA_GROUP = GQA_HEADS // GQA_KV_HEADS
NA_WIDTH = NA_HEADS * HEAD_DIM
GQA_WIDTH = GQA_HEADS * HEAD_DIM
GQA_KV_WIDTH = GQA_KV_HEADS * HEAD_DIM
MIX_WIDTH = NA_WIDTH + GQA_WIDTH
IN_WIDTH = 3 * NA_WIDTH + GQA_WIDTH + 2 * GQA_KV_WIDTH
IN_SPLITS = (NA_WIDTH, 2 * NA_WIDTH, 3 * NA_WIDTH, 3 * NA_WIDTH + GQA_WIDTH,
             3 * NA_WIDTH + GQA_WIDTH + GQA_KV_WIDTH)
NA_WIN_H = 8
NA_WIN_W = 16
RPB_H = 2 * NA_WIN_H - 1
RPB_W = 2 * NA_WIN_W - 1
D_FF = 2816
N_MOD = 9
Q_BLOCK = 128
ROPE_THETA = 10000.0
EPS = 1e-6
NEG_INF = -1e30

kernel_name = 'hybrid_na_gqa_macaron_dit'


def rms_norm(x, g):
    xf = x.astype(jnp.float32)
    y = xf * lax.rsqrt(jnp.mean(xf * xf, axis=-1, keepdims=True) + EPS)
    return (y * g.astype(jnp.float32)).astype(x.dtype)


def modulate(h, shift, scale):
    return h * (1 + scale) + shift


def ada_mod(cond, w_mod, b_mod):
    m = (jax.nn.silu(cond) @ w_mod + b_mod)[..., None, :]
    return jnp.split(m, N_MOD, axis=-1)


def swiglu(h, w13, w2):
    a, g = jnp.split(h @ w13, 2, axis=-1)
    return (jax.nn.silu(g) * a) @ w2


def ffn_half(x, norm_g, shift, scale, gate, w13, w2):
    h = modulate(rms_norm(x, norm_g), shift, scale)
    return x + 0.5 * gate * swiglu(h, w13, w2)


def axial_rope_tables(seq_len, dtype):
    t = jnp.arange(seq_len, dtype=jnp.int32)
    row = (t // GRID_W).astype(jnp.float32)
    col = (t % GRID_W).astype(jnp.float32)
    axis_dim = HEAD_DIM // 2
    inv_freq = ROPE_THETA ** (-jnp.arange(0, axis_dim, 2, dtype=jnp.float32) / axis_dim)
    ang_r = row[:, None] * inv_freq[None, :]
    ang_c = col[:, None] * inv_freq[None, :]
    return (jnp.cos(ang_r).astype(dtype), jnp.sin(ang_r).astype(dtype),
            jnp.cos(ang_c).astype(dtype), jnp.sin(ang_c).astype(dtype))


def rope_1d(x, cos, sin):
    x1, x2 = jnp.split(x, 2, axis=-1)
    return jnp.concatenate([x1 * cos - x2 * sin, x2 * cos + x1 * sin], axis=-1)


def apply_axial_rope(x, rope):
    cos_r, sin_r, cos_c, sin_c = rope
    xr, xc = jnp.split(x, 2, axis=-1)
    return jnp.concatenate([rope_1d(xr, cos_r, sin_r), rope_1d(xc, cos_c, sin_c)], axis=-1)


def split_heads(t, n_heads):
    b, l, _ = t.shape
    return t.reshape(b, l, n_heads, HEAD_DIM).transpose(0, 2, 1, 3)


def merge_heads(t):
    b, h, l, d = t.shape
    return t.transpose(0, 2, 1, 3).reshape(b, l, h * d)


def combined_projection(h, w_in):
    qa, ka, va, qb, kb, vb = jnp.split(h @ w_in, IN_SPLITS, axis=-1)
    return (split_heads(qa, NA_HEADS), split_heads(ka, NA_HEADS), split_heads(va, NA_HEADS),
            split_heads(qb, GQA_HEADS), split_heads(kb, GQA_KV_HEADS), split_heads(vb, GQA_KV_HEADS))


def neighbourhood_attention(q, k, v, k_ctx, v_ctx, rpb):
    b, h, l, d = q.shape
    rows = l // GRID_W
    kh = min(NA_WIN_H, rows)
    scale = d ** -0.5
    qg = q.reshape(b, h, rows, GRID_W, d)
    kg = k.reshape(b, h, rows, GRID_W, d)
    vg = v.reshape(b, h, rows, GRID_W, d)
    r = jnp.arange(rows)
    row_start = jnp.clip(r - kh // 2, 0, rows - kh)
    key_rows = row_start[:, None] + jnp.arange(kh)[None, :]
    k_blk = kg[:, :, key_rows]
    v_blk = vg[:, :, key_rows]
    cq = jnp.arange(GRID_W)
    col_start = jnp.clip(cq - NA_WIN_W // 2, 0, GRID_W - NA_WIN_W)
    in_win = (cq[None, :] >= col_start[:, None]) & (cq[None, :] < col_start[:, None] + NA_WIN_W)
    row_off = key_rows - r[:, None]
    col_off = jnp.clip(cq[None, :] - cq[:, None] + NA_WIN_W - 1, 0, RPB_W - 1)
    bias = rpb[:, (row_off + NA_WIN_H - 1)[:, None, :, None], col_off[None, :, None, :]]
    s_loc = jnp.einsum('bhrqd,bhrnkd->bhrqnk', qg, k_blk).astype(jnp.float32) * scale
    s_loc = s_loc + bias[None].astype(jnp.float32)
    s_loc = jnp.where(in_win[:, None, :], s_loc, NEG_INF)
    s_loc = s_loc.reshape(b, h, rows, GRID_W, kh * GRID_W)
    s_ctx = jnp.einsum('bhrqd,bhcd->bhrqc', qg, k_ctx).astype(jnp.float32) * scale
    p = jax.nn.softmax(jnp.concatenate([s_loc, s_ctx], axis=-1), axis=-1).astype(v.dtype)
    p_loc = p[..., :kh * GRID_W].reshape(b, h, rows, GRID_W, kh, GRID_W)
    p_ctx = p[..., kh * GRID_W:]
    o = (jnp.einsum('bhrqnk,bhrnkd->bhrqd', p_loc, v_blk)
         + jnp.einsum('bhrqc,bhcd->bhrqd', p_ctx, v_ctx))
    return o.reshape(b, h, l, d)


def gqa_blocked_attention(q, k_all, v_all):
    b, kvh, g, l, d = q.shape
    scale = d ** -0.5
    nb = l // Q_BLOCK
    qb = q.reshape(b, kvh, g, nb, Q_BLOCK, d).transpose(3, 0, 1, 2, 4, 5)

    def one_block(q_blk):
        s = jnp.einsum('bkgqd,bknd->bkgqn', q_blk, k_all).astype(jnp.float32) * scale
        p = jax.nn.softmax(s, axis=-1).astype(v_all.dtype)
        return jnp.einsum('bkgqn,bknd->bkgqd', p, v_all)

    o = lax.map(one_block, qb)
    return o.transpose(1, 2, 3, 0, 4, 5).reshape(b, kvh, g, l, d)


def dense_group_attention(q, k, v):
    s = jnp.einsum('bkgqd,bkcd->bkgqc', q, k).astype(jnp.float32) * (q.shape[-1] ** -0.5)
    p = jax.nn.softmax(s, axis=-1).astype(v.dtype)
    return jnp.einsum('bkgqc,bkcd->bkgqd', p, v)


def merge_groups(o_a, o_b, g_a, g_b, w_out):
    y = jnp.concatenate([rms_norm(merge_heads(o_a), g_a), rms_norm(merge_heads(o_b), g_b)], axis=-1)
    return y @ w_out


def mixer_sublayer(x, x_ctx, mod, mod_ctx, norm_g, w_in, rpb, q_norm, k_norm, g_a, g_b, w_out, rope, update_ctx):
    shift, scale, gate = mod
    shift_c, scale_c, gate_c = mod_ctx
    h = modulate(rms_norm(x, norm_g), shift, scale)
    h_c = modulate(rms_norm(x_ctx, norm_g), shift_c, scale_c)
    qa, ka, va, qb, kb, vb = combined_projection(h, w_in)
    qa_c, ka_c, va_c, qb_c, kb_c, vb_c = combined_projection(h_c, w_in)
    kb_c = rms_norm(kb_c, k_norm)
    o_a = neighbourhood_attention(qa, ka, va, ka_c, va_c, rpb)
    qb = apply_axial_rope(rms_norm(qb, q_norm), rope)
    kb = apply_axial_rope(rms_norm(kb, k_norm), rope)
    b, _, l, _ = qb.shape
    k_all = jnp.concatenate([kb, kb_c], axis=2)
    v_all = jnp.concatenate([vb, vb_c], axis=2)
    o_b = gqa_blocked_attention(qb.reshape(b, GQA_KV_HEADS, GQA_GROUP, l, HEAD_DIM), k_all, v_all)
    o_b = o_b.reshape(b, GQA_HEADS, l, HEAD_DIM)
    x = x + gate * merge_groups(o_a, o_b, g_a, g_b, w_out)
    if update_ctx:
        cl = x_ctx.shape[1]
        o_a_c = dense_group_attention(qa_c[:, :, None], ka_c, va_c)[:, :, 0]
        o_b_c = dense_group_attention(
            rms_norm(qb_c, q_norm).reshape(b, GQA_KV_HEADS, GQA_GROUP, cl, HEAD_DIM), kb_c, vb_c)
        o_b_c = o_b_c.reshape(b, GQA_HEADS, cl, HEAD_DIM)
        x_ctx = x_ctx + gate_c * merge_groups(o_a_c, o_b_c, g_a, g_b, w_out)
    return x, x_ctx


def setup_inputs(seed: int = 0) -> dict:
    key = jax.random.key(seed)
    ks = jax.random.split(key, 24)
    f32 = jnp.float32

    def nrm(k, shape, s):
        return jax.random.normal(k, shape, f32) * s

    def gain(k, shape):
        return 1.0 + 0.05 * jax.random.normal(k, shape, f32)

    return {
        'x': nrm(ks[0], (BATCH, SEQ, D_MODEL), 1.0),
        'c': nrm(ks[1], (BATCH, D_MODEL), 1.0),
        'ctx': nrm(ks[2], (BATCH, CTX_LEN, D_MODEL), 1.0),
        'c_ctx': nrm(ks[3], (D_MODEL,), 1.0),
        'w_mod': nrm(ks[4], (DEPTH, D_MODEL, N_MOD * D_MODEL), 0.5 * D_MODEL ** -0.5),
        'b_mod': nrm(ks[5], (DEPTH, N_MOD * D_MODEL), 0.01),
        'norm_ffn1': gain(ks[6], (DEPTH, D_MODEL)),
        'ffn1_w13': nrm(ks[7], (DEPTH, D_MODEL, 2 * D_FF), D_MODEL ** -0.5),
        'ffn1_w2': nrm(ks[8], (DEPTH, D_FF, D_MODEL), D_FF ** -0.5),
        'norm_mix': gain(ks[9], (DEPTH, D_MODEL)),
        'w_in': nrm(ks[10], (DEPTH, D_MODEL, IN_WIDTH), D_MODEL ** -0.5),
        'na_rpb': nrm(ks[11], (DEPTH, NA_HEADS, RPB_H, RPB_W), 0.1),
        'gqa_q_norm': gain(ks[12], (DEPTH, HEAD_DIM)),
        'gqa_k_norm': gain(ks[13], (DEPTH, HEAD_DIM)),
        'out_norm_a': gain(ks[14], (DEPTH, NA_WIDTH)),
        'out_norm_b': gain(ks[15], (DEPTH, GQA_WIDTH)),
        'w_out': nrm(ks[16], (DEPTH, MIX_WIDTH, D_MODEL), MIX_WIDTH ** -0.5),
        'norm_ffn2': gain(ks[17], (DEPTH, D_MODEL)),
        'ffn2_w13': nrm(ks[18], (DEPTH, D_MODEL, 2 * D_FF), D_MODEL ** -0.5),
        'ffn2_w2': nrm(ks[19], (DEPTH, D_FF, D_MODEL), D_FF ** -0.5),
        'norm_f': gain(ks[20], (D_MODEL,)),
    }


def reference(x, c, ctx, c_ctx, w_mod, b_mod, norm_ffn1, ffn1_w13, ffn1_w2, norm_mix, w_in, na_rpb,
              gqa_q_norm, gqa_k_norm, out_norm_a, out_norm_b, w_out, norm_ffn2, ffn2_w13, ffn2_w2, norm_f):
    rope = axial_rope_tables(x.shape[1], x.dtype)
    x_ctx = ctx
    for i in range(DEPTH):
        last = i == DEPTH - 1
        m = ada_mod(c, w_mod[i], b_mod[i])
        mc = ada_mod(c_ctx, w_mod[i], b_mod[i])
        x = ffn_half(x, norm_ffn1[i], m[0], m[1], m[2], ffn1_w13[i], ffn1_w2[i])
        x_ctx = ffn_half(x_ctx, norm_ffn1[i], mc[0], mc[1], mc[2], ffn1_w13[i], ffn1_w2[i])
        x, x_ctx = mixer_sublayer(x, x_ctx, m[3:6], mc[3:6], norm_mix[i], w_in[i], na_rpb[i],
                                  gqa_q_norm[i], gqa_k_norm[i], out_norm_a[i], out_norm_b[i], w_out[i],
                                  rope, not last)
        x = ffn_half(x, norm_ffn2[i], m[6], m[7], m[8], ffn2_w13[i], ffn2_w2[i])
        if not last:
            x_ctx = ffn_half(x_ctx, norm_ffn2[i], mc[6], mc[7], mc[8], ffn2_w13[i], ffn2_w2[i])
    return rms_norm(x, norm_f)
```

```python
import functools

import numpy as np
import jax
import jax.numpy as jnp
from jax import lax
from jax.experimental import pallas as pl
from jax.experimental.pallas import tpu as pltpu

F32 = jnp.float32
BF16 = jnp.bfloat16

D_MODEL = 1024
BATCH = 8
SEQ = 2048
DEPTH = 2
GRID_W = 64
GRID_H = SEQ // GRID_W
CTX_LEN = 256
HEAD_DIM = 64
NA_HEADS = 8
GQA_HEADS = 8
GQA_KV_HEADS = 2
GQA_GROUP = GQA_HEADS // GQA_KV_HEADS
NA_WIDTH = NA_HEADS * HEAD_DIM
GQA_WIDTH = GQA_HEADS * HEAD_DIM
GQA_KV_WIDTH = GQA_KV_HEADS * HEAD_DIM
IN_WIDTH = 3 * NA_WIDTH + GQA_WIDTH + 2 * GQA_KV_WIDTH
NA_WIN_H = 8
NA_WIN_W = 16
RPB_H = 2 * NA_WIN_H - 1
RPB_W = 2 * NA_WIN_W - 1
D_FF = 2816
N_MOD = 9
ROPE_THETA = 10000.0
EPS = 1e-6
NEG_INF = -1e30
SM_SCALE = HEAD_DIM ** -0.5

N_GROUPS = BATCH + 1
CTX_GROUP = BATCH
COND_ROWS = 16

V7X_LANES = 128
V7X_VMEM_BYTES = 64 * 1024 * 1024
V7X_VMEM_REQUEST_CAP = 56 * 1024 * 1024

ROW_TILE = 512
FF_CHUNK = 256
N_FF_CHUNKS = D_FF // FF_CHUNK
MOD_TILE = 1152
NA_Q_ROWS = 4
NA_Q_TILE = NA_Q_ROWS * GRID_W
NA_K_ROWS = NA_Q_ROWS + NA_WIN_H - 1
NA_K_TILE = NA_K_ROWS * GRID_W
NA_BLOCKS = GRID_H // NA_Q_ROWS
GQA_Q_TILE = 128

assert BATCH * CTX_LEN == SEQ
assert D_FF % FF_CHUNK == 0 and SEQ % ROW_TILE == 0 and (N_MOD * D_MODEL) % MOD_TILE == 0
assert GRID_H % NA_Q_ROWS == 0 and SEQ % GQA_Q_TILE == 0


def _vmem_limit(block_bytes, scratch_bytes, temp_bytes):
    need = 2 * block_bytes + scratch_bytes + temp_bytes
    return int(min(max(need, 16 * 1024 * 1024), V7X_VMEM_REQUEST_CAP))


def _rms(x):
    return x * lax.rsqrt(jnp.mean(x * x, axis=-1, keepdims=True) + EPS)


def _dot(a, b):
    return jnp.dot(a, b, preferred_element_type=F32)


def _dot_nt(a, b):
    return lax.dot_general(a, b, (((1,), (1,)), ((), ())), preferred_element_type=F32)


def _mod_kernel(c_ref, w_ref, b_ref, o_ref):
    c = c_ref[...]
    s = (c * jax.nn.sigmoid(c)).astype(BF16)
    o_ref[0] = _dot(s, w_ref[0].astype(BF16)) + b_ref[0]


def _modulation(cond, w_mod, b_mod):
    n = N_MOD * D_MODEL
    return pl.pallas_call(
        _mod_kernel,
        grid=(DEPTH, n // MOD_TILE),
        in_specs=[
            pl.BlockSpec((COND_ROWS, D_MODEL), lambda l, j: (0, 0)),
            pl.BlockSpec((1, D_MODEL, MOD_TILE), lambda l, j: (l, 0, j)),
            pl.BlockSpec((1, 1, MOD_TILE), lambda l, j: (l, 0, j)),
        ],
        out_specs=pl.BlockSpec((1, COND_ROWS, MOD_TILE), lambda l, j: (l, 0, j)),
        out_shape=jax.ShapeDtypeStruct((DEPTH, COND_ROWS, n), F32),
        compiler_params=pltpu.CompilerParams(
            dimension_semantics=("parallel", "parallel"),
            vmem_limit_bytes=_vmem_limit(D_MODEL * MOD_TILE * 4, 0, D_MODEL * MOD_TILE * 2)),
        name="adaln_modulation",
    )(cond, w_mod, b_mod.reshape(DEPTH, 1, n))


def _ffn_kernel(x_ref, mod_ref, g_ref, w13_ref, w2_ref, gf_ref, o_ref, u_ref, *, final):
    x = x_ref[0]
    m = mod_ref[0]
    shift, scale, gate = m[:, :D_MODEL], m[:, D_MODEL:2 * D_MODEL], m[:, 2 * D_MODEL:]
    h = ((_rms(x) * g_ref[...]) * (1.0 + scale) + shift).astype(BF16)
    for j in range(N_FF_CHUNKS):
        a = _dot(h, w13_ref[j])
        g = _dot(h, w13_ref[N_FF_CHUNKS + j])
        u_ref[:, j * FF_CHUNK:(j + 1) * FF_CHUNK] = ((g * jax.nn.sigmoid(g)) * a).astype(BF16)
    y = _dot(u_ref[...], w2_ref[...])
    out = x + (0.5 * gate) * y
    if final:
        out = _rms(out) * gf_ref[...]
    o_ref[0] = out


def _ffn_half(x, mod, sub, norm_g, w13c, w2b, n_groups, final_g=None):
    final = final_g is not None
    gf = final_g if final else norm_g
    blk = ROW_TILE * D_MODEL * 4
    wbytes = (2 * N_FF_CHUNKS * D_MODEL * FF_CHUNK + D_FF * D_MODEL) * 2
    return pl.pallas_call(
        functools.partial(_ffn_kernel, final=final),
        grid=(n_groups, SEQ // ROW_TILE),
        in_specs=[
            pl.BlockSpec((1, ROW_TILE, D_MODEL), lambda g, i: (g, i, 0)),
            pl.BlockSpec((1, 1, 3 * D_MODEL), lambda g, i: (g, 0, sub)),
            pl.BlockSpec((1, D_MODEL), lambda g, i: (0, 0)),
            pl.BlockSpec((2 * N_FF_CHUNKS, D_MODEL, FF_CHUNK), lambda g, i: (0, 0, 0),
                         pipeline_mode=pl.Buffered(1)),
            pl.BlockSpec((D_FF, D_MODEL), lambda g, i: (0, 0), pipeline_mode=pl.Buffered(1)),
            pl.BlockSpec((1, D_MODEL), lambda g, i: (0, 0)),
        ],
        out_specs=pl.BlockSpec((1, ROW_TILE, D_MODEL), lambda g, i: (g, i, 0)),
        out_shape=jax.ShapeDtypeStruct((n_groups, SEQ, D_MODEL), F32),
        scratch_shapes=[pltpu.VMEM((ROW_TILE, D_FF), BF16)],
        compiler_params=pltpu.CompilerParams(
            dimension_semantics=("parallel", "parallel"),
            vmem_limit_bytes=_vmem_limit(2 * blk, wbytes + ROW_TILE * D_FF * 2, 3 * blk)),
        name="ffn_half_final" if final else "ffn_half",
    )(x, mod, norm_g.reshape(1, D_MODEL), w13c, w2b, gf.reshape(1, D_MODEL))


def _norm_rope(x, ones_blk, gain, cos, sin, bit16):
    sq = x * x
    hi = sq.astype(BF16)
    lo = (sq - hi.astype(F32)).astype(BF16)
    ssum = _dot(hi, ones_blk) + _dot(lo, ones_blk)
    y = (x * lax.rsqrt(ssum * (1.0 / HEAD_DIM) + EPS)) * gain
    quarter = HEAD_DIM // 4
    partner = jnp.where(bit16, pltpu.roll(y, quarter, 1), pltpu.roll(y, V7X_LANES - quarter, 1))
    return y * cos + partner * sin


def _proj_kernel(x_ref, mod_ref, g_ref, w_ref, ones_ref, gq_ref, gk_ref, cos_ref, sin_ref,
                 qa_ref, ka_ref, va_ref, qb_ref, kb_ref, vb_ref):
    x = x_ref[0]
    m = mod_ref[0]
    shift, scale = m[:, :D_MODEL], m[:, D_MODEL:2 * D_MODEL]
    h = ((_rms(x) * g_ref[...]) * (1.0 + scale) + shift).astype(BF16)
    qa_ref[0] = (_dot(h, w_ref[:, 0:NA_WIDTH]) * SM_SCALE).astype(BF16)
    ka_ref[0] = _dot(h, w_ref[:, NA_WIDTH:2 * NA_WIDTH]).astype(BF16)
    va_ref[0] = _dot(h, w_ref[:, 2 * NA_WIDTH:3 * NA_WIDTH]).astype(BF16)
    qb0 = 3 * NA_WIDTH
    kb0 = qb0 + GQA_WIDTH
    vb0 = kb0 + GQA_KV_WIDTH
    vb_ref[0] = _dot(h, w_ref[:, vb0:vb0 + GQA_KV_WIDTH]).astype(BF16)
    cos = cos_ref[0]
    sin = sin_ref[0]
    ones_blk = ones_ref[...]
    lane = lax.broadcasted_iota(jnp.int32, (ROW_TILE, V7X_LANES), 1)
    bit16 = (lane & (HEAD_DIM // 4)) != 0
    gq = gq_ref[...] * SM_SCALE
    gk = gk_ref[...]
    for c in range(GQA_WIDTH // V7X_LANES):
        q = _dot(h, w_ref[:, qb0 + c * V7X_LANES:qb0 + (c + 1) * V7X_LANES])
        qb_ref[0, :, c * V7X_LANES:(c + 1) * V7X_LANES] = _norm_rope(
            q, ones_blk, gq, cos, sin, bit16).astype(BF16)
    k = _dot(h, w_ref[:, kb0:kb0 + GQA_KV_WIDTH])
    kb_ref[0] = _norm_rope(k, ones_blk, gk, cos, sin, bit16).astype(BF16)


def _in_projection(x, mod, norm_g, w_in_b, ones_blk, gq, gk, cos_t, sin_t):
    blk_in = ROW_TILE * D_MODEL * 4
    blk_out = ROW_TILE * IN_WIDTH * 2
    row = lambda g, i: (g, i, 0)
    tab = lambda g, i: (jnp.where(g == CTX_GROUP, 1, 0), i, 0)
    const2 = lambda g, i: (0, 0)
    shapes = [jax.ShapeDtypeStruct((N_GROUPS, SEQ, w), BF16)
              for w in (NA_WIDTH, NA_WIDTH, NA_WIDTH, GQA_WIDTH, GQA_KV_WIDTH, GQA_KV_WIDTH)]
    return pl.pallas_call(
        _proj_kernel,
        grid=(N_GROUPS, SEQ // ROW_TILE),
        in_specs=[
            pl.BlockSpec((1, ROW_TILE, D_MODEL), row),
            pl.BlockSpec((1, 1, 3 * D_MODEL), lambda g, i: (g, 0, 1)),
            pl.BlockSpec((1, D_MODEL), const2),
            pl.BlockSpec((D_MODEL, IN_WIDTH), const2, pipeline_mode=pl.Buffered(1)),
            pl.BlockSpec((V7X_LANES, V7X_LANES), const2),
            pl.BlockSpec((1, V7X_LANES), const2),
            pl.BlockSpec((1, V7X_LANES), const2),
            pl.BlockSpec((1, ROW_TILE, V7X_LANES), tab),
            pl.BlockSpec((1, ROW_TILE, V7X_LANES), tab),
        ],
        out_specs=[pl.BlockSpec((1, ROW_TILE, s.shape[-1]), row) for s in shapes],
        out_shape=shapes,
        compiler_params=pltpu.CompilerParams(
            dimension_semantics=("parallel", "parallel"),
            vmem_limit_bytes=_vmem_limit(blk_in + blk_out, D_MODEL * IN_WIDTH * 2, 3 * blk_in)),
        name="in_projection",
    )(x, mod, norm_g.reshape(1, D_MODEL), w_in_b, ones_blk, gq, gk, cos_t, sin_t)


def _softmax_pv(s_list, v_list):
    m = s_list[0].max(axis=-1, keepdims=True)
    for s in s_list[1:]:
        m = jnp.maximum(m, s.max(axis=-1, keepdims=True))
    l = None
    o = None
    for s, v in zip(s_list, v_list):
        e = jnp.exp(s - m)
        ls = e.sum(axis=-1, keepdims=True)
        os_ = _dot(e.astype(BF16), v)
        l = ls if l is None else l + ls
        o = os_ if o is None else o + os_
    return o / l


def _pair_heads(q, fn):
    lane = lax.broadcasted_iota(jnp.int32, q.shape, 1)
    low = lane < HEAD_DIM
    zero = jnp.zeros_like(q)
    o_lo = fn(jnp.where(low, q, zero))
    o_hi = fn(jnp.where(low, zero, q))
    return jnp.where(lax.broadcasted_iota(jnp.int32, o_lo.shape, 1) < HEAD_DIM, o_lo, o_hi)


def _na_kernel(q_ref, k_ref, v_ref, kc_ref, vc_ref, bias_ref, o_ref):
    i = pl.program_id(2)
    first_key_row = jnp.clip(NA_Q_ROWS * i - NA_WIN_H // 2, 0, GRID_H - NA_K_ROWS)
    k0 = pl.multiple_of(first_key_row * GRID_W, GRID_W)
    kw = k_ref[0, pl.ds(k0, NA_K_TILE), :]
    vw = v_ref[0, pl.ds(k0, NA_K_TILE), :]
    kc = kc_ref[0]
    vc = vc_ref[0]
    q = q_ref[0]
    lane = lax.broadcasted_iota(jnp.int32, q.shape, 1)
    low = lane < HEAD_DIM
    zero = jnp.zeros_like(q)
    outs = []
    for hh in range(2):
        qm = jnp.where(low, q, zero) if hh == 0 else jnp.where(low, zero, q)
        s_loc = _dot_nt(qm, kw) + bias_ref[0, hh]
        s_ctx = _dot_nt(qm, kc)
        outs.append(_softmax_pv([s_loc, s_ctx], [vw, vc]))
    o_ref[0] = jnp.where(low, outs[0], outs[1])


def _na_bias_table(rpb):
    ridx, cidx, valid = [], [], []
    for i in (0, 1, NA_BLOCKS - 1):
        first_key_row = min(max(NA_Q_ROWS * i - NA_WIN_H // 2, 0), GRID_H - NA_K_ROWS)
        ql = np.arange(NA_Q_TILE)
        kl = np.arange(NA_K_TILE)
        qr = (NA_Q_ROWS * i + ql // GRID_W)[:, None]
        qc = (ql % GRID_W)[:, None]
        kr = (first_key_row + kl // GRID_W)[None, :]
        kc = (kl % GRID_W)[None, :]
        rs = np.clip(qr - NA_WIN_H // 2, 0, GRID_H - NA_WIN_H)
        cs = np.clip(qc - NA_WIN_W // 2, 0, GRID_W - NA_WIN_W)
        ok = (kr >= rs) & (kr < rs + NA_WIN_H) & (kc >= cs) & (kc < cs + NA_WIN_W)
        ridx.append(np.clip(kr - qr + NA_WIN_H - 1, 0, RPB_H - 1) + 0 * kc)
        cidx.append(np.clip(kc - qc + NA_WIN_W - 1, 0, RPB_W - 1) + 0 * kr)
        valid.append(ok)
    ridx, cidx, valid = np.stack(ridx), np.stack(cidx), np.stack(valid)
    t = jnp.where(valid[None], rpb[:, ridx, cidx], NEG_INF)
    return jnp.transpose(t, (1, 0, 2, 3)).astype(F32)


def _na_attention(qa, ka, va, bias):
    pairs = NA_WIDTH // V7X_LANES
    variant = lambda b, hp, i: (jnp.where(i == 0, 0, jnp.where(i == NA_BLOCKS - 1, 2, 1)), hp, 0, 0)
    blk = (NA_Q_TILE * V7X_LANES * 2 + 2 * SEQ * V7X_LANES * 2 + 2 * CTX_LEN * V7X_LANES * 2
           + 2 * NA_Q_TILE * NA_K_TILE * 4 + NA_Q_TILE * V7X_LANES * 4)
    return pl.pallas_call(
        _na_kernel,
        grid=(BATCH, pairs, NA_BLOCKS),
        in_specs=[
            pl.BlockSpec((1, NA_Q_TILE, V7X_LANES), lambda b, hp, i: (b, i, hp)),
            pl.BlockSpec((1, SEQ, V7X_LANES), lambda b, hp, i: (b, 0, hp)),
            pl.BlockSpec((1, SEQ, V7X_LANES), lambda b, hp, i: (b, 0, hp)),
            pl.BlockSpec((1, CTX_LEN, V7X_LANES), lambda b, hp, i: (CTX_GROUP, b, hp)),
            pl.BlockSpec((1, CTX_LEN, V7X_LANES), lambda b, hp, i: (CTX_GROUP, b, hp)),
            pl.BlockSpec((1, 2, NA_Q_TILE, NA_K_TILE), variant),
        ],
        out_specs=pl.BlockSpec((1, NA_Q_TILE, V7X_LANES), lambda b, hp, i: (b, i, hp)),
        out_shape=jax.ShapeDtypeStruct((N_GROUPS, SEQ, NA_WIDTH), F32),
        compiler_params=pltpu.CompilerParams(
            dimension_semantics=("parallel", "parallel", "parallel"),
            vmem_limit_bytes=_vmem_limit(blk, 0, 8 * NA_Q_TILE * (NA_K_TILE + CTX_LEN) * 4)),
        name="neighbourhood_attention",
    )(qa, ka, va, ka, va, bias)


def _gqa_core(q, k_list, v_list, n_q):
    cols = [q[:, c * V7X_LANES:(c + 1) * V7X_LANES] for c in range(GQA_GROUP)]
    lane = lax.broadcasted_iota(jnp.int32, (n_q, V7X_LANES), 1)
    low = lane < HEAD_DIM
    zero = jnp.zeros((n_q, V7X_LANES), q.dtype)
    outs = []
    for kvh in range(GQA_KV_HEADS):
        stack = jnp.concatenate(
            [jnp.where(low, c, zero) if kvh == 0 else jnp.where(low, zero, c) for c in cols], axis=0)
        s_list = [_dot_nt(stack, k) for k in k_list]
        outs.append(_softmax_pv(s_list, v_list))
    return [jnp.where(low, outs[0][c * n_q:(c + 1) * n_q], outs[1][c * n_q:(c + 1) * n_q])
            for c in range(GQA_GROUP)]


def _gqa_kernel(q_ref, k_ref, v_ref, kc_ref, vc_ref, o_ref):
    res = _gqa_core(q_ref[0], [k_ref[0], kc_ref[0]], [v_ref[0], vc_ref[0]], GQA_Q_TILE)
    for c in range(GQA_GROUP):
        o_ref[0, :, c * V7X_LANES:(c + 1) * V7X_LANES] = res[c]


def _gqa_attention(qb, kb, vb):
    blk = (GQA_Q_TILE * GQA_WIDTH * (2 + 4) + 2 * (SEQ + CTX_LEN) * V7X_LANES * 2)
    scores = GQA_GROUP * GQA_Q_TILE * (SEQ + CTX_LEN) * 4
    return pl.pallas_call(
        _gqa_kernel,
        grid=(BATCH, SEQ // GQA_Q_TILE),
        in_specs=[
            pl.BlockSpec((1, GQA_Q_TILE, GQA_WIDTH), lambda b, i: (b, i, 0)),
            pl.BlockSpec((1, SEQ, GQA_KV_WIDTH), lambda b, i: (b, 0, 0)),
            pl.BlockSpec((1, SEQ, GQA_KV_WIDTH), lambda b, i: (b, 0, 0)),
            pl.BlockSpec((1, CTX_LEN, GQA_KV_WIDTH), lambda b, i: (CTX_GROUP, b, 0)),
            pl.BlockSpec((1, CTX_LEN, GQA_KV_WIDTH), lambda b, i: (CTX_GROUP, b, 0)),
        ],
        out_specs=pl.BlockSpec((1, GQA_Q_TILE, GQA_WIDTH), lambda b, i: (b, i, 0)),
        out_shape=jax.ShapeDtypeStruct((N_GROUPS, SEQ, GQA_WIDTH), F32),
        compiler_params=pltpu.CompilerParams(
            dimension_semantics=("parallel", "parallel"),
            vmem_limit_bytes=_vmem_limit(blk, 0, 6 * scores)),
        name="gqa_attention",
    )(qb, kb, vb, kb, vb)


def _ctx_kernel(qa_ref, ka_ref, va_ref, qb_ref, kb_ref, vb_ref, oa_in, ob_in, oa_ref, ob_ref):
    del oa_in, ob_in
    for c in range(NA_WIDTH // V7X_LANES):
        sl = slice(c * V7X_LANES, (c + 1) * V7X_LANES)
        k = ka_ref[0, :, sl]
        v = va_ref[0, :, sl]
        oa_ref[0, :, sl] = _pair_heads(qa_ref[0, :, sl], lambda qm: _softmax_pv([_dot_nt(qm, k)], [v]))
    res = _gqa_core(qb_ref[0], [kb_ref[0]], [vb_ref[0]], CTX_LEN)
    for c in range(GQA_GROUP):
        ob_ref[0, :, c * V7X_LANES:(c + 1) * V7X_LANES] = res[c]


def _ctx_attention(qa, ka, va, qb, kb, vb, oa, ob):
    row = lambda b: (CTX_GROUP, b, 0)
    wide = pl.BlockSpec((1, CTX_LEN, NA_WIDTH), row)
    narrow = pl.BlockSpec((1, CTX_LEN, GQA_KV_WIDTH), row)
    anyspec = pl.BlockSpec(memory_space=pl.ANY)
    blk = CTX_LEN * (4 * NA_WIDTH * 2 + 2 * GQA_KV_WIDTH * 2 + 2 * NA_WIDTH * 4)
    return pl.pallas_call(
        _ctx_kernel,
        grid=(BATCH,),
        in_specs=[wide, wide, wide, wide, narrow, narrow, anyspec, anyspec],
        out_specs=[wide, wide],
        out_shape=[jax.ShapeDtypeStruct(oa.shape, F32), jax.ShapeDtypeStruct(ob.shape, F32)],
        input_output_aliases={6: 0, 7: 1},
        compiler_params=pltpu.CompilerParams(
            dimension_semantics=("parallel",),
            vmem_limit_bytes=_vmem_limit(blk, 0, 16 * GQA_GROUP * CTX_LEN * CTX_LEN * 4)),
        name="context_attention",
    )(qa, ka, va, qb, kb, vb, oa, ob)


def _outproj_kernel(x_ref, oa_ref, ob_ref, mod_ref, ga_ref, gb_ref, w_ref, o_ref):
    gate = mod_ref[0][:, 2 * D_MODEL:]
    ya = (_rms(oa_ref[0]) * ga_ref[...]).astype(BF16)
    yb = (_rms(ob_ref[0]) * gb_ref[...]).astype(BF16)
    y = _dot(ya, w_ref[:NA_WIDTH, :]) + _dot(yb, w_ref[NA_WIDTH:, :])
    o_ref[0] = x_ref[0] + gate * y


def _out_projection(x, oa, ob, mod, ga, gb, w_out_b, n_groups):
    row = lambda g, i: (g, i, 0)
    const2 = lambda g, i: (0, 0)
    blk = ROW_TILE * (2 * D_MODEL + NA_WIDTH + GQA_WIDTH) * 4
    return pl.pallas_call(
        _outproj_kernel,
        grid=(n_groups, SEQ // ROW_TILE),
        in_specs=[
            pl.BlockSpec((1, ROW_TILE, D_MODEL), row),
            pl.BlockSpec((1, ROW_TILE, NA_WIDTH), row),
            pl.BlockSpec((1, ROW_TILE, GQA_WIDTH), row),
            pl.BlockSpec((1, 1, 3 * D_MODEL), lambda g, i: (g, 0, 1)),
            pl.BlockSpec((1, NA_WIDTH), const2),
            pl.BlockSpec((1, GQA_WIDTH), const2),
            pl.BlockSpec((NA_WIDTH + GQA_WIDTH, D_MODEL), const2, pipeline_mode=pl.Buffered(1)),
        ],
        out_specs=pl.BlockSpec((1, ROW_TILE, D_MODEL), row),
        out_shape=jax.ShapeDtypeStruct((n_groups, SEQ, D_MODEL), F32),
        compiler_params=pltpu.CompilerParams(
            dimension_semantics=("parallel", "parallel"),
            vmem_limit_bytes=_vmem_limit(blk, (NA_WIDTH + GQA_WIDTH) * D_MODEL * 2, blk)),
        name="out_projection",
    )(x, oa, ob, mod, ga.reshape(1, NA_WIDTH), gb.reshape(1, GQA_WIDTH), w_out_b)


def _gqa_head_permutation():
    p = np.arange(GQA_WIDTH)
    col, half, d = p // V7X_LANES, (p % V7X_LANES) // HEAD_DIM, p % HEAD_DIM
    return HEAD_DIM * (col + GQA_GROUP * half) + d


def _rope_tables():
    t = jnp.arange(SEQ, dtype=jnp.int32)
    row = (t // GRID_W).astype(F32)
    col = (t % GRID_W).astype(F32)
    axis_dim = HEAD_DIM // 2
    inv_freq = ROPE_THETA ** (-jnp.arange(0, axis_dim, 2, dtype=F32) / axis_dim)
    ang_r = row[:, None] * inv_freq[None, :]
    ang_c = col[:, None] * inv_freq[None, :]
    cos_r, sin_r, cos_c, sin_c = jnp.cos(ang_r), jnp.sin(ang_r), jnp.cos(ang_c), jnp.sin(ang_c)
    cos_h = jnp.concatenate([cos_r, cos_r, cos_c, cos_c], axis=-1)
    sin_h = jnp.concatenate([-sin_r, sin_r, -sin_c, sin_c], axis=-1)
    heads = V7X_LANES // HEAD_DIM
    cos2 = jnp.tile(cos_h, (1, heads))
    sin2 = jnp.tile(sin_h, (1, heads))
    return (jnp.stack([cos2, jnp.ones_like(cos2)]), jnp.stack([sin2, jnp.zeros_like(sin2)]))


def _chunk_w13(w13):
    w = w13.astype(BF16).reshape(D_MODEL, 2 * N_FF_CHUNKS, FF_CHUNK)
    return jnp.transpose(w, (1, 0, 2))


def kernel(x, c, ctx, c_ctx, w_mod, b_mod, norm_ffn1, ffn1_w13, ffn1_w2, norm_mix, w_in, na_rpb,
           gqa_q_norm, gqa_k_norm, out_norm_a, out_norm_b, w_out, norm_ffn2, ffn2_w13, ffn2_w2, norm_f):
    perm = _gqa_head_permutation()
    cos_t, sin_t = _rope_tables()
    head_of_lane = np.arange(V7X_LANES) // HEAD_DIM
    ones_blk = jnp.asarray(head_of_lane[:, None] == head_of_lane[None, :], BF16)
    heads_per_col = V7X_LANES // HEAD_DIM

    cond = jnp.zeros((COND_ROWS, D_MODEL), F32).at[:BATCH].set(c).at[CTX_GROUP].set(c_ctx)
    mod_all = _modulation(cond, w_mod, b_mod)

    xs = jnp.concatenate([x, ctx.reshape(1, SEQ, D_MODEL)], axis=0)

    for l in range(DEPTH):
        last = l == DEPTH - 1
        mod = mod_all[l].reshape(COND_ROWS, 1, N_MOD * D_MODEL)
        qb0 = 3 * NA_WIDTH
        w_in_l = w_in[l]
        w_in_b = jnp.concatenate(
            [w_in_l[:, :qb0], w_in_l[:, qb0:qb0 + GQA_WIDTH][:, perm], w_in_l[:, qb0 + GQA_WIDTH:]],
            axis=1).astype(BF16)
        w_out_l = w_out[l]
        w_out_b = jnp.concatenate([w_out_l[:NA_WIDTH], w_out_l[NA_WIDTH:][perm]], axis=0).astype(BF16)
        gq = jnp.tile(gqa_q_norm[l], heads_per_col).reshape(1, V7X_LANES)
        gk = jnp.tile(gqa_k_norm[l], heads_per_col).reshape(1, V7X_LANES)

        xs = _ffn_half(xs, mod, 0, norm_ffn1[l], _chunk_w13(ffn1_w13[l]), ffn1_w2[l].astype(BF16),
                       N_GROUPS)
        qa, ka, va, qb, kb, vb = _in_projection(xs, mod, norm_mix[l], w_in_b, ones_blk, gq, gk,
                                                cos_t, sin_t)
        oa = _na_attention(qa, ka, va, _na_bias_table(na_rpb[l]))
        ob = _gqa_attention(qb, kb, vb)
        n_groups = BATCH if last else N_GROUPS
        if not last:
            oa, ob = _ctx_attention(qa, ka, va, qb, kb, vb, oa, ob)
        xs = _out_projection(xs, oa, ob, mod, out_norm_a[l], out_norm_b[l][perm], w_out_b, n_groups)
        xs = _ffn_half(xs, mod, 2, norm_ffn2[l], _chunk_w13(ffn2_w13[l]), ffn2_w2[l].astype(BF16),
                       n_groups, final_g=norm_f if last else None)
    return xs
```

```python
import functools

import numpy as np
import jax
import jax.numpy as jnp
from jax import lax
from jax.experimental import pallas as pl
from jax.experimental.pallas import tpu as pltpu

F32 = jnp.float32
BF16 = jnp.bfloat16

D_MODEL = 1024
BATCH = 8
SEQ = 2048
DEPTH = 2
GRID_W = 64
GRID_H = SEQ // GRID_W
CTX_LEN = 256
HEAD_DIM = 64
NA_HEADS = 8
GQA_HEADS = 8
GQA_KV_HEADS = 2
GQA_GROUP = GQA_HEADS // GQA_KV_HEADS
NA_WIDTH = NA_HEADS * HEAD_DIM
GQA_WIDTH = GQA_HEADS * HEAD_DIM
GQA_KV_WIDTH = GQA_KV_HEADS * HEAD_DIM
IN_WIDTH = 3 * NA_WIDTH + GQA_WIDTH + 2 * GQA_KV_WIDTH
NA_WIN_H = 8
NA_WIN_W = 16
RPB_H = 2 * NA_WIN_H - 1
RPB_W = 2 * NA_WIN_W - 1
D_FF = 2816
N_MOD = 9
ROPE_THETA = 10000.0
EPS = 1e-6
NEG_INF = -1e30
SM_SCALE = HEAD_DIM ** -0.5

N_GROUPS = BATCH + 1
CTX_GROUP = BATCH
COND_ROWS = 16

V7X_LANES = 128
V7X_VMEM_BYTES = 64 * 1024 * 1024
V7X_VMEM_REQUEST_CAP = 56 * 1024 * 1024

ROW_TILE = 512
FF_CHUNK = 256
N_FF_CHUNKS = D_FF // FF_CHUNK
MOD_TILE = 1152
NA_Q_ROWS = 4
NA_Q_TILE = NA_Q_ROWS * GRID_W
NA_K_ROWS = NA_Q_ROWS + NA_WIN_H - 1
NA_K_TILE = NA_K_ROWS * GRID_W
NA_BLOCKS = GRID_H // NA_Q_ROWS
GQA_Q_TILE = 512

assert BATCH * CTX_LEN == SEQ
assert D_FF % FF_CHUNK == 0 and SEQ % ROW_TILE == 0 and (N_MOD * D_MODEL) % MOD_TILE == 0
assert GRID_H % NA_Q_ROWS == 0 and SEQ % GQA_Q_TILE == 0


def _vmem_limit(block_bytes, scratch_bytes, temp_bytes):
    need = 2 * block_bytes + scratch_bytes + temp_bytes
    return int(min(max(need, 16 * 1024 * 1024), V7X_VMEM_REQUEST_CAP))


def _rms(x):
    return x * lax.rsqrt(jnp.mean(x * x, axis=-1, keepdims=True) + EPS)


def _dot(a, b):
    return jnp.dot(a, b, preferred_element_type=F32)


def _dot_nt(a, b):
    return lax.dot_general(a, b, (((1,), (1,)), ((), ())), preferred_element_type=F32)


def _mod_kernel(c_ref, w_ref, b_ref, o_ref):
    c = c_ref[...]
    s = (c * jax.nn.sigmoid(c)).astype(BF16)
    o_ref[0] = _dot(s, w_ref[0].astype(BF16)) + b_ref[0]


def _modulation(cond, w_mod, b_mod):
    n = N_MOD * D_MODEL
    return pl.pallas_call(
        _mod_kernel,
        grid=(DEPTH, n // MOD_TILE),
        in_specs=[
            pl.BlockSpec((COND_ROWS, D_MODEL), lambda l, j: (0, 0)),
            pl.BlockSpec((1, D_MODEL, MOD_TILE), lambda l, j: (l, 0, j)),
            pl.BlockSpec((1, 1, MOD_TILE), lambda l, j: (l, 0, j)),
        ],
        out_specs=pl.BlockSpec((1, COND_ROWS, MOD_TILE), lambda l, j: (l, 0, j)),
        out_shape=jax.ShapeDtypeStruct((DEPTH, COND_ROWS, n), F32),
        compiler_params=pltpu.CompilerParams(
            dimension_semantics=("parallel", "parallel"),
            vmem_limit_bytes=_vmem_limit(D_MODEL * MOD_TILE * 4, 0, D_MODEL * MOD_TILE * 2)),
        name="adaln_modulation",
    )(cond, w_mod, b_mod.reshape(DEPTH, 1, n))


def _ffn_kernel(*refs, final, split_ctx):
    if split_ctx:
        x_ref, ctx_ref, mod_ref, g_ref, w13_ref, w2_ref, gf_ref, o_ref, u_ref = refs
        x = jnp.where(pl.program_id(0) == CTX_GROUP, ctx_ref[0], x_ref[0])
    else:
        x_ref, mod_ref, g_ref, w13_ref, w2_ref, gf_ref, o_ref, u_ref = refs
        x = x_ref[0]
    m = mod_ref[0]
    shift, scale, gate = m[:, :D_MODEL], m[:, D_MODEL:2 * D_MODEL], m[:, 2 * D_MODEL:]
    h = ((_rms(x) * g_ref[...]) * (1.0 + scale) + shift).astype(BF16)
    for j in range(N_FF_CHUNKS):
        a = _dot(h, w13_ref[:, j * FF_CHUNK:(j + 1) * FF_CHUNK])
        g = _dot(h, w13_ref[:, D_FF + j * FF_CHUNK:D_FF + (j + 1) * FF_CHUNK])
        u_ref[:, j * FF_CHUNK:(j + 1) * FF_CHUNK] = ((g * jax.nn.sigmoid(g)) * a).astype(BF16)
    y = _dot(u_ref[...], w2_ref[...])
    out = x + (0.5 * gate) * y
    if final:
        out = _rms(out) * gf_ref[...]
    o_ref[0] = out


def _ffn_half(x, mod, sub, norm_g, w13b, w2b, n_groups, ctx=None, final_g=None):
    final = final_g is not None
    split_ctx = ctx is not None
    gf = final_g if final else norm_g
    blk = ROW_TILE * D_MODEL * 4
    wbytes = 3 * D_FF * D_MODEL * 2
    last_tile = SEQ // ROW_TILE - 1
    if split_ctx:
        x_specs = [
            pl.BlockSpec((1, ROW_TILE, D_MODEL),
                         lambda g, i: (jnp.minimum(g, BATCH - 1), jnp.where(g == CTX_GROUP, last_tile, i), 0)),
            pl.BlockSpec((1, ROW_TILE, D_MODEL), lambda g, i: (0, jnp.where(g == CTX_GROUP, i, 0), 0)),
        ]
        xs = (x, ctx)
    else:
        x_specs = [pl.BlockSpec((1, ROW_TILE, D_MODEL), lambda g, i: (g, i, 0))]
        xs = (x,)
    return pl.pallas_call(
        functools.partial(_ffn_kernel, final=final, split_ctx=split_ctx),
        grid=(n_groups, SEQ // ROW_TILE),
        in_specs=x_specs + [
            pl.BlockSpec((1, 1, 3 * D_MODEL), lambda g, i: (g, 0, sub)),
            pl.BlockSpec((1, D_MODEL), lambda g, i: (0, 0)),
            pl.BlockSpec((D_MODEL, 2 * D_FF), lambda g, i: (0, 0), pipeline_mode=pl.Buffered(1)),
            pl.BlockSpec((D_FF, D_MODEL), lambda g, i: (0, 0), pipeline_mode=pl.Buffered(1)),
            pl.BlockSpec((1, D_MODEL), lambda g, i: (0, 0)),
        ],
        out_specs=pl.BlockSpec((1, ROW_TILE, D_MODEL), lambda g, i: (g, i, 0)),
        out_shape=jax.ShapeDtypeStruct((n_groups, SEQ, D_MODEL), F32),
        scratch_shapes=[pltpu.VMEM((ROW_TILE, D_FF), BF16)],
        compiler_params=pltpu.CompilerParams(
            dimension_semantics=("arbitrary", "arbitrary"),
            vmem_limit_bytes=_vmem_limit((2 + split_ctx) * blk, wbytes + ROW_TILE * D_FF * 2, 3 * blk)),
        name="ffn_half_final" if final else ("ffn_half_first" if split_ctx else "ffn_half"),
    )(*xs, mod, norm_g.reshape(1, D_MODEL), w13b, w2b, gf.reshape(1, D_MODEL))


def _norm_rope(x, ones2, gain, cos, sin, bit16):
    sq = x * x
    hi = sq.astype(BF16)
    lo = (sq - hi.astype(F32)).astype(BF16)
    ssum = _dot(jnp.concatenate([hi, lo], axis=1), ones2)
    y = (x * lax.rsqrt(ssum * (1.0 / HEAD_DIM) + EPS)) * gain
    quarter = HEAD_DIM // 4
    partner = jnp.where(bit16, pltpu.roll(y, quarter, 1), pltpu.roll(y, V7X_LANES - quarter, 1))
    return y * cos + partner * sin


def _proj_kernel(x_ref, mod_ref, g_ref, w_ref, ones_ref, gq_ref, gk_ref, cos_ref, sin_ref,
                 qa_ref, ka_ref, va_ref, qb_ref, kb_ref, vb_ref):
    x = x_ref[0]
    m = mod_ref[0]
    shift, scale = m[:, :D_MODEL], m[:, D_MODEL:2 * D_MODEL]
    h = ((_rms(x) * g_ref[...]) * (1.0 + scale) + shift).astype(BF16)
    qb0 = 3 * NA_WIDTH
    kb0 = qb0 + GQA_WIDTH
    cos = cos_ref[0]
    sin = sin_ref[0]
    ones2 = ones_ref[...]
    lane = lax.broadcasted_iota(jnp.int32, (ROW_TILE, V7X_LANES), 1)
    bit16 = (lane & (HEAD_DIM // 4)) != 0
    gq = gq_ref[...] * SM_SCALE
    gk = gk_ref[...]
    q = _dot(h, w_ref[:, qb0:kb0])
    for c in range(GQA_WIDTH // V7X_LANES):
        sl = slice(c * V7X_LANES, (c + 1) * V7X_LANES)
        qb_ref[0, :, sl] = _norm_rope(q[:, sl], ones2, gq, cos, sin, bit16).astype(BF16)
    kv = _dot(h, w_ref[:, kb0:kb0 + 2 * GQA_KV_WIDTH])
    kb_ref[0] = _norm_rope(kv[:, :GQA_KV_WIDTH], ones2, gk, cos, sin, bit16).astype(BF16)
    vb_ref[0] = kv[:, GQA_KV_WIDTH:].astype(BF16)
    qa_ref[0] = (_dot(h, w_ref[:, 0:NA_WIDTH]) * SM_SCALE).astype(BF16)
    ka_ref[0] = _dot(h, w_ref[:, NA_WIDTH:2 * NA_WIDTH]).astype(BF16)
    va_ref[0] = _dot(h, w_ref[:, 2 * NA_WIDTH:3 * NA_WIDTH]).astype(BF16)


def _in_projection(x, mod, norm_g, w_in_b, ones2, gq, gk, cos_t, sin_t):
    blk_in = ROW_TILE * D_MODEL * 4
    blk_out = ROW_TILE * IN_WIDTH * 2
    row = lambda g, i: (g, i, 0)
    tab = lambda g, i: (jnp.where(g == CTX_GROUP, 1, 0), i, 0)
    const2 = lambda g, i: (0, 0)
    shapes = [jax.ShapeDtypeStruct((N_GROUPS, SEQ, w), BF16)
              for w in (NA_WIDTH, NA_WIDTH, NA_WIDTH, GQA_WIDTH, GQA_KV_WIDTH, GQA_KV_WIDTH)]
    return pl.pallas_call(
        _proj_kernel,
        grid=(N_GROUPS, SEQ // ROW_TILE),
        in_specs=[
            pl.BlockSpec((1, ROW_TILE, D_MODEL), row),
            pl.BlockSpec((1, 1, 3 * D_MODEL), lambda g, i: (g, 0, 1)),
            pl.BlockSpec((1, D_MODEL), const2),
            pl.BlockSpec((D_MODEL, IN_WIDTH), const2, pipeline_mode=pl.Buffered(1)),
            pl.BlockSpec((2 * V7X_LANES, V7X_LANES), const2),
            pl.BlockSpec((1, V7X_LANES), const2),
            pl.BlockSpec((1, V7X_LANES), const2),
            pl.BlockSpec((1, ROW_TILE, V7X_LANES), tab),
            pl.BlockSpec((1, ROW_TILE, V7X_LANES), tab),
        ],
        out_specs=[pl.BlockSpec((1, ROW_TILE, s.shape[-1]), row) for s in shapes],
        out_shape=shapes,
        compiler_params=pltpu.CompilerParams(
            dimension_semantics=("parallel", "parallel"),
            vmem_limit_bytes=_vmem_limit(blk_in + blk_out, D_MODEL * IN_WIDTH * 2, 3 * blk_in)),
        name="in_projection",
    )(x, mod, norm_g.reshape(1, D_MODEL), w_in_b, ones2, gq, gk, cos_t, sin_t)


def _softmax_pv(s_list, v_list):
    m = s_list[0].max(axis=-1, keepdims=True)
    for s in s_list[1:]:
        m = jnp.maximum(m, s.max(axis=-1, keepdims=True))
    l = None
    o = None
    for s, v in zip(s_list, v_list):
        e = jnp.exp(s - m)
        ls = e.sum(axis=-1, keepdims=True)
        os_ = _dot(e.astype(BF16), v)
        l = ls if l is None else l + ls
        o = os_ if o is None else o + os_
    return o / l


def _pair_heads(q, fn):
    low = lax.broadcasted_iota(jnp.int32, q.shape, 1) < HEAD_DIM
    zero = jnp.zeros_like(q)
    o_lo = fn(jnp.where(low, q, zero), 0)
    o_hi = fn(jnp.where(low, zero, q), 1)
    return jnp.where(low, o_lo, o_hi)


def _na_kernel(q_ref, k_ref, v_ref, kc_ref, vc_ref, bias_ref, o_ref):
    i = pl.program_id(1)
    first_key_row = jnp.clip(NA_Q_ROWS * i - NA_WIN_H // 2, 0, GRID_H - NA_K_ROWS)
    k0 = pl.multiple_of(first_key_row * GRID_W, GRID_W)
    for c in range(NA_WIDTH // V7X_LANES):
        sl = slice(c * V7X_LANES, (c + 1) * V7X_LANES)
        kw = k_ref[0, pl.ds(k0, NA_K_TILE), sl]
        vw = v_ref[0, pl.ds(k0, NA_K_TILE), sl]
        kc = kc_ref[0, :, sl]
        vc = vc_ref[0, :, sl]

        def one_head(qm, hh, c=c, kw=kw, vw=vw, kc=kc, vc=vc):
            s_loc = _dot_nt(qm, kw) + bias_ref[0, 2 * c + hh]
            return _softmax_pv([s_loc, _dot_nt(qm, kc)], [vw, vc])

        o_ref[0, :, sl] = _pair_heads(q_ref[0, :, sl], one_head)


def _na_bias_table(rpb):
    period = 2 * GRID_W
    ring = jnp.concatenate(
        [rpb[:, :, NA_WIN_W - 1:], jnp.zeros((NA_HEADS, RPB_H, period - RPB_W), F32),
         rpb[:, :, :NA_WIN_W - 1]], axis=-1)
    flat = jnp.tile(ring, (1, 1, GRID_W))[:, :, :GRID_W * (period - 1)]
    toe = flat.reshape(NA_HEADS, RPB_H, GRID_W, period - 1)[..., :GRID_W]
    qc = np.arange(GRID_W)[:, None]
    kc = np.arange(GRID_W)[None, :]
    cs = np.clip(qc - NA_WIN_W // 2, 0, GRID_W - NA_WIN_W)
    in_win = (kc >= cs) & (kc < cs + NA_WIN_W)
    masked = jnp.full((NA_HEADS, 1, GRID_W, GRID_W), NEG_INF, F32)
    blocks = jnp.concatenate([jnp.where(in_win, toe, NEG_INF), masked], axis=1)
    variants = []
    for i in (0, 1, NA_BLOCKS - 1):
        first_key_row = min(max(NA_Q_ROWS * i - NA_WIN_H // 2, 0), GRID_H - NA_K_ROWS)
        q_rows = []
        for a in range(NA_Q_ROWS):
            qr = NA_Q_ROWS * i + a
            rs = min(max(qr - NA_WIN_H // 2, 0), GRID_H - NA_WIN_H)
            picks = []
            for n in range(NA_K_ROWS):
                kr = first_key_row + n
                picks.append(kr - qr + NA_WIN_H - 1 if rs <= kr < rs + NA_WIN_H else RPB_H)
            q_rows.append(jnp.concatenate([blocks[:, p] for p in picks], axis=-1))
        variants.append(jnp.concatenate(q_rows, axis=1))
    return jnp.stack(variants)


def _na_attention(qa, ka, va, bias):
    variant = lambda b, i: (jnp.where(i == 0, 0, jnp.where(i == NA_BLOCKS - 1, 2, 1)), 0, 0, 0)
    blk = (NA_Q_TILE * NA_WIDTH * (2 + 4) + 2 * (SEQ + CTX_LEN) * NA_WIDTH * 2
           + NA_HEADS * NA_Q_TILE * NA_K_TILE * 4)
    scores = NA_HEADS * NA_Q_TILE * (NA_K_TILE + CTX_LEN) * 4
    return pl.pallas_call(
        _na_kernel,
        grid=(BATCH, NA_BLOCKS),
        in_specs=[
            pl.BlockSpec((1, NA_Q_TILE, NA_WIDTH), lambda b, i: (b, i, 0)),
            pl.BlockSpec((1, SEQ, NA_WIDTH), lambda b, i: (b, 0, 0)),
            pl.BlockSpec((1, SEQ, NA_WIDTH), lambda b, i: (b, 0, 0)),
            pl.BlockSpec((1, CTX_LEN, NA_WIDTH), lambda b, i: (CTX_GROUP, b, 0)),
            pl.BlockSpec((1, CTX_LEN, NA_WIDTH), lambda b, i: (CTX_GROUP, b, 0)),
            pl.BlockSpec((1, NA_HEADS, NA_Q_TILE, NA_K_TILE), variant),
        ],
        out_specs=pl.BlockSpec((1, NA_Q_TILE, NA_WIDTH), lambda b, i: (b, i, 0)),
        out_shape=jax.ShapeDtypeStruct((N_GROUPS, SEQ, NA_WIDTH), F32),
        compiler_params=pltpu.CompilerParams(
            dimension_semantics=("parallel", "arbitrary"),
            vmem_limit_bytes=_vmem_limit(blk, 0, 2 * scores)),
        name="neighbourhood_attention",
    )(qa, ka, va, ka, va, bias)


def _gqa_core(q, kt_list, v_list, n_q):
    cols = [q[:, c * V7X_LANES:(c + 1) * V7X_LANES] for c in range(GQA_GROUP)]
    lane = lax.broadcasted_iota(jnp.int32, (n_q, V7X_LANES), 1)
    low = lane < HEAD_DIM
    zero = jnp.zeros((n_q, V7X_LANES), q.dtype)
    outs = []
    for kvh in range(GQA_KV_HEADS):
        stack = jnp.concatenate(
            [jnp.where(low, c, zero) if kvh == 0 else jnp.where(low, zero, c) for c in cols], axis=0)
        s_list = [_dot(stack, kt) for kt in kt_list]
        outs.append(_softmax_pv(s_list, v_list))
    return [jnp.where(low, outs[0][c * n_q:(c + 1) * n_q], outs[1][c * n_q:(c + 1) * n_q])
            for c in range(GQA_GROUP)]


GQA_SUB_TILE = 128


def _gqa_kernel(q_ref, kt_ref, v_ref, kct_ref, vc_ref, o_ref):
    kts = [kt_ref[0], kct_ref[0]]
    vs = [v_ref[0], vc_ref[0]]
    for t in range(GQA_Q_TILE // GQA_SUB_TILE):
        rows = slice(t * GQA_SUB_TILE, (t + 1) * GQA_SUB_TILE)
        res = _gqa_core(q_ref[0, rows, :], kts, vs, GQA_SUB_TILE)
        for c in range(GQA_GROUP):
            o_ref[0, rows, c * V7X_LANES:(c + 1) * V7X_LANES] = res[c]


def _gqa_attention(qb, kbt, vb):
    blk = (GQA_Q_TILE * GQA_WIDTH * (2 + 4) + 2 * (SEQ + CTX_LEN) * V7X_LANES * 2)
    scores = GQA_KV_HEADS * GQA_GROUP * GQA_Q_TILE * (SEQ + CTX_LEN) * 4
    return pl.pallas_call(
        _gqa_kernel,
        grid=(BATCH, SEQ // GQA_Q_TILE),
        in_specs=[
            pl.BlockSpec((1, GQA_Q_TILE, GQA_WIDTH), lambda b, i: (b, i, 0)),
            pl.BlockSpec((1, GQA_KV_WIDTH, SEQ), lambda b, i: (b, 0, 0)),
            pl.BlockSpec((1, SEQ, GQA_KV_WIDTH), lambda b, i: (b, 0, 0)),
            pl.BlockSpec((1, GQA_KV_WIDTH, CTX_LEN), lambda b, i: (CTX_GROUP, 0, b)),
            pl.BlockSpec((1, CTX_LEN, GQA_KV_WIDTH), lambda b, i: (CTX_GROUP, b, 0)),
        ],
        out_specs=pl.BlockSpec((1, GQA_Q_TILE, GQA_WIDTH), lambda b, i: (b, i, 0)),
        out_shape=jax.ShapeDtypeStruct((N_GROUPS, SEQ, GQA_WIDTH), F32),
        compiler_params=pltpu.CompilerParams(
            dimension_semantics=("parallel", "arbitrary"),
            vmem_limit_bytes=_vmem_limit(blk, 0, 2 * scores)),
        name="gqa_attention",
    )(qb, kbt, vb, kbt, vb)


def _ctx_kernel(qa_ref, ka_ref, va_ref, qb_ref, kbt_ref, vb_ref, oa_in, ob_in, oa_ref, ob_ref):
    del oa_in, ob_in
    for c in range(NA_WIDTH // V7X_LANES):
        sl = slice(c * V7X_LANES, (c + 1) * V7X_LANES)
        k = ka_ref[0, :, sl]
        v = va_ref[0, :, sl]
        oa_ref[0, :, sl] = _pair_heads(
            qa_ref[0, :, sl], lambda qm, hh, k=k, v=v: _softmax_pv([_dot_nt(qm, k)], [v]))
    res = _gqa_core(qb_ref[0], [kbt_ref[0]], [vb_ref[0]], CTX_LEN)
    for c in range(GQA_GROUP):
        ob_ref[0, :, c * V7X_LANES:(c + 1) * V7X_LANES] = res[c]


def _ctx_attention(qa, ka, va, qb, kbt, vb, oa, ob):
    row = lambda b: (CTX_GROUP, b, 0)
    wide = pl.BlockSpec((1, CTX_LEN, NA_WIDTH), row)
    narrow = pl.BlockSpec((1, CTX_LEN, GQA_KV_WIDTH), row)
    narrow_t = pl.BlockSpec((1, GQA_KV_WIDTH, CTX_LEN), lambda b: (CTX_GROUP, 0, b))
    anyspec = pl.BlockSpec(memory_space=pl.ANY)
    blk = CTX_LEN * (4 * NA_WIDTH * 2 + 2 * GQA_KV_WIDTH * 2 + 2 * NA_WIDTH * 4)
    return pl.pallas_call(
        _ctx_kernel,
        grid=(BATCH,),
        in_specs=[wide, wide, wide, wide, narrow_t, narrow, anyspec, anyspec],
        out_specs=[wide, wide],
        out_shape=[jax.ShapeDtypeStruct(oa.shape, F32), jax.ShapeDtypeStruct(ob.shape, F32)],
        input_output_aliases={6: 0, 7: 1},
        compiler_params=pltpu.CompilerParams(
            dimension_semantics=("parallel",),
            vmem_limit_bytes=_vmem_limit(blk, 0, 16 * GQA_GROUP * CTX_LEN * CTX_LEN * 4)),
        name="context_attention",
    )(qa, ka, va, qb, kbt, vb, oa, ob)


def _outproj_kernel(x_ref, oa_ref, ob_ref, mod_ref, ga_ref, gb_ref, w_ref, o_ref):
    gate = mod_ref[0][:, 2 * D_MODEL:]
    ya = (_rms(oa_ref[0]) * ga_ref[...]).astype(BF16)
    yb = (_rms(ob_ref[0]) * gb_ref[...]).astype(BF16)
    y = _dot(ya, w_ref[:NA_WIDTH, :]) + _dot(yb, w_ref[NA_WIDTH:, :])
    o_ref[0] = x_ref[0] + gate * y


def _out_projection(x, oa, ob, mod, ga, gb, w_out_b, n_groups):
    row = lambda g, i: (g, i, 0)
    const2 = lambda g, i: (0, 0)
    blk = ROW_TILE * (2 * D_MODEL + NA_WIDTH + GQA_WIDTH) * 4
    return pl.pallas_call(
        _outproj_kernel,
        grid=(n_groups, SEQ // ROW_TILE),
        in_specs=[
            pl.BlockSpec((1, ROW_TILE, D_MODEL), row),
            pl.BlockSpec((1, ROW_TILE, NA_WIDTH), row),
            pl.BlockSpec((1, ROW_TILE, GQA_WIDTH), row),
            pl.BlockSpec((1, 1, 3 * D_MODEL), lambda g, i: (g, 0, 1)),
            pl.BlockSpec((1, NA_WIDTH), const2),
            pl.BlockSpec((1, GQA_WIDTH), const2),
            pl.BlockSpec((NA_WIDTH + GQA_WIDTH, D_MODEL), const2, pipeline_mode=pl.Buffered(1)),
        ],
        out_specs=pl.BlockSpec((1, ROW_TILE, D_MODEL), row),
        out_shape=jax.ShapeDtypeStruct((n_groups, SEQ, D_MODEL), F32),
        compiler_params=pltpu.CompilerParams(
            dimension_semantics=("parallel", "parallel"),
            vmem_limit_bytes=_vmem_limit(blk, (NA_WIDTH + GQA_WIDTH) * D_MODEL * 2, blk)),
        name="out_projection",
    )(x, oa, ob, mod, ga.reshape(1, NA_WIDTH), gb.reshape(1, GQA_WIDTH), w_out_b)


def _gqa_head_permutation():
    p = np.arange(GQA_WIDTH)
    col, half, d = p // V7X_LANES, (p % V7X_LANES) // HEAD_DIM, p % HEAD_DIM
    return HEAD_DIM * (col + GQA_GROUP * half) + d


def _rope_tables():
    t = jnp.arange(SEQ, dtype=jnp.int32)
    row = (t // GRID_W).astype(F32)
    col = (t % GRID_W).astype(F32)
    axis_dim = HEAD_DIM // 2
    inv_freq = ROPE_THETA ** (-jnp.arange(0, axis_dim, 2, dtype=F32) / axis_dim)
    ang_r = row[:, None] * inv_freq[None, :]
    ang_c = col[:, None] * inv_freq[None, :]
    cos_r, sin_r, cos_c, sin_c = jnp.cos(ang_r), jnp.sin(ang_r), jnp.cos(ang_c), jnp.sin(ang_c)
    cos_h = jnp.concatenate([cos_r, cos_r, cos_c, cos_c], axis=-1)
    sin_h = jnp.concatenate([-sin_r, sin_r, -sin_c, sin_c], axis=-1)
    heads = V7X_LANES // HEAD_DIM
    cos2 = jnp.tile(cos_h, (1, heads))
    sin2 = jnp.tile(sin_h, (1, heads))
    return (jnp.stack([cos2, jnp.ones_like(cos2)]), jnp.stack([sin2, jnp.zeros_like(sin2)]))


def kernel(x, c, ctx, c_ctx, w_mod, b_mod, norm_ffn1, ffn1_w13, ffn1_w2, norm_mix, w_in, na_rpb,
           gqa_q_norm, gqa_k_norm, out_norm_a, out_norm_b, w_out, norm_ffn2, ffn2_w13, ffn2_w2, norm_f):
    perm = _gqa_head_permutation()
    cos_t, sin_t = _rope_tables()
    head_of_lane = np.arange(V7X_LANES) // HEAD_DIM
    ones_blk = np.asarray(head_of_lane[:, None] == head_of_lane[None, :], np.float32)
    ones2 = jnp.asarray(np.concatenate([ones_blk, ones_blk], axis=0), BF16)
    heads_per_col = V7X_LANES // HEAD_DIM

    cond = jnp.zeros((COND_ROWS, D_MODEL), F32).at[:BATCH].set(c).at[CTX_GROUP].set(c_ctx)
    mod_all = _modulation(cond, w_mod, b_mod)

    xs = x
    for l in range(DEPTH):
        last = l == DEPTH - 1
        mod = mod_all[l].reshape(COND_ROWS, 1, N_MOD * D_MODEL)
        qb0 = 3 * NA_WIDTH
        w_in_l = w_in[l]
        w_in_b = jnp.concatenate(
            [w_in_l[:, :qb0], w_in_l[:, qb0:qb0 + GQA_WIDTH][:, perm], w_in_l[:, qb0 + GQA_WIDTH:]],
            axis=1).astype(BF16)
        w_out_l = w_out[l]
        w_out_b = jnp.concatenate([w_out_l[:NA_WIDTH], w_out_l[NA_WIDTH:][perm]], axis=0).astype(BF16)
        gq = jnp.tile(gqa_q_norm[l], heads_per_col).reshape(1, V7X_LANES)
        gk = jnp.tile(gqa_k_norm[l], heads_per_col).reshape(1, V7X_LANES)

        xs = _ffn_half(xs, mod, 0, norm_ffn1[l], ffn1_w13[l].astype(BF16), ffn1_w2[l].astype(BF16),
                       N_GROUPS, ctx=ctx.reshape(1, SEQ, D_MODEL) if l == 0 else None)
        qa, ka, va, qb, kb, vb = _in_projection(xs, mod, norm_mix[l], w_in_b, ones2, gq, gk,
                                                cos_t, sin_t)
        oa = _na_attention(qa, ka, va, _na_bias_table(na_rpb[l]))
        kbt = jnp.swapaxes(kb, 1, 2)
        ob = _gqa_attention(qb, kbt, vb)
        n_groups = BATCH if last else N_GROUPS
        if not last:
            oa, ob = _ctx_attention(qa, ka, va, qb, kbt, vb, oa, ob)
        xs = _out_projection(xs, oa, ob, mod, out_norm_a[l], out_norm_b[l][perm], w_out_b, n_groups)
        xs = _ffn_half(xs, mod, 2, norm_ffn2[l], ffn2_w13[l].astype(BF16), ffn2_w2[l].astype(BF16),
                       n_groups, final_g=norm_f if last else None)
    return xs
```

```python
import functools

import numpy as np
import jax
import jax.numpy as jnp
from jax import lax
from jax.experimental import pallas as pl
from jax.experimental.pallas import tpu as pltpu

F32 = jnp.float32
BF16 = jnp.bfloat16

D_MODEL = 1024
BATCH = 8
SEQ = 2048
DEPTH = 2
GRID_W = 64
GRID_H = SEQ // GRID_W
CTX_LEN = 256
HEAD_DIM = 64
NA_HEADS = 8
GQA_HEADS = 8
GQA_KV_HEADS = 2
GQA_GROUP = GQA_HEADS // GQA_KV_HEADS
NA_WIDTH = NA_HEADS * HEAD_DIM
GQA_WIDTH = GQA_HEADS * HEAD_DIM
GQA_KV_WIDTH = GQA_KV_HEADS * HEAD_DIM
IN_WIDTH = 3 * NA_WIDTH + GQA_WIDTH + 2 * GQA_KV_WIDTH
NA_WIN_H = 8
NA_WIN_W = 16
RPB_H = 2 * NA_WIN_H - 1
RPB_W = 2 * NA_WIN_W - 1
D_FF = 2816
N_MOD = 9
ROPE_THETA = 10000.0
EPS = 1e-6
NEG_INF = -1e30
LOG2_E = 1.4426950408889634
Q_SCALE = LOG2_E * HEAD_DIM ** -0.5

N_GROUPS = BATCH + 1
CTX_GROUP = BATCH
COND_ROWS = 16

V7X_LANES = 128
V7X_VMEM_BYTES = 64 * 1024 * 1024
V7X_VMEM_REQUEST_CAP = 56 * 1024 * 1024

ROW_TILE = 512
FF_CHUNK = 256
N_FF_CHUNKS = D_FF // FF_CHUNK
MOD_TILE = 1152
NA_Q_ROWS = 4
NA_Q_TILE = NA_Q_ROWS * GRID_W
NA_K_ROWS = NA_Q_ROWS + NA_WIN_H - 1
NA_K_TILE = NA_K_ROWS * GRID_W
NA_BLOCKS = GRID_H // NA_Q_ROWS
NA_TILES = BATCH * NA_BLOCKS
GQA_Q_TILE = 256
GQA_SUB_TILE = 64
GQA_TILES = BATCH * (SEQ // GQA_Q_TILE)
GQA_KEY_CHUNK = 256

assert BATCH * CTX_LEN == SEQ
assert D_FF % FF_CHUNK == 0 and SEQ % ROW_TILE == 0 and (N_MOD * D_MODEL) % MOD_TILE == 0
assert GRID_H % NA_Q_ROWS == 0 and SEQ % GQA_Q_TILE == 0 and GQA_Q_TILE % GQA_SUB_TILE == 0
assert SEQ % GQA_KEY_CHUNK == 0 and CTX_LEN % GQA_KEY_CHUNK == 0


def _vmem_limit(block_bytes, scratch_bytes, temp_bytes):
    need = 2 * block_bytes + scratch_bytes + temp_bytes
    return int(min(max(need, 16 * 1024 * 1024), V7X_VMEM_REQUEST_CAP))


def _rms(x):
    return x * lax.rsqrt(jnp.mean(x * x, axis=-1, keepdims=True) + EPS)


def _dot(a, b):
    return jnp.dot(a, b, preferred_element_type=F32)


def _dot_nt(a, b):
    return lax.dot_general(a, b, (((1,), (1,)), ((), ())), preferred_element_type=F32)


def _mod_kernel(c_ref, w_ref, b_ref, o_ref):
    c = c_ref[...]
    s = (c * jax.nn.sigmoid(c)).astype(BF16)
    o_ref[0] = _dot(s, w_ref[0].astype(BF16)) + b_ref[0]


def _modulation(cond, w_mod, b_mod):
    n = N_MOD * D_MODEL
    return pl.pallas_call(
        _mod_kernel,
        grid=(DEPTH, n // MOD_TILE),
        in_specs=[
            pl.BlockSpec((COND_ROWS, D_MODEL), lambda l, j: (0, 0)),
            pl.BlockSpec((1, D_MODEL, MOD_TILE), lambda l, j: (l, 0, j)),
            pl.BlockSpec((1, 1, MOD_TILE), lambda l, j: (l, 0, j)),
        ],
        out_specs=pl.BlockSpec((1, COND_ROWS, MOD_TILE), lambda l, j: (l, 0, j)),
        out_shape=jax.ShapeDtypeStruct((DEPTH, COND_ROWS, n), F32),
        compiler_params=pltpu.CompilerParams(
            dimension_semantics=("parallel", "parallel"),
            vmem_limit_bytes=_vmem_limit(D_MODEL * MOD_TILE * 4, 0, D_MODEL * MOD_TILE * 2)),
        name="adaln_modulation",
    )(cond, w_mod, b_mod.reshape(DEPTH, 1, n))


def _ffn_kernel(*refs, final, split_ctx):
    if split_ctx:
        x_ref, ctx_ref, mod_ref, g_ref, w13_ref, w2_ref, gf_ref, o_ref, u_ref = refs
        x = jnp.where(pl.program_id(0) == CTX_GROUP, ctx_ref[0], x_ref[0])
    else:
        x_ref, mod_ref, g_ref, w13_ref, w2_ref, gf_ref, o_ref, u_ref = refs
        x = x_ref[0]
    m = mod_ref[0]
    shift, scale, gate = m[:, :D_MODEL], m[:, D_MODEL:2 * D_MODEL], m[:, 2 * D_MODEL:]
    h = ((_rms(x) * g_ref[...]) * (1.0 + scale) + shift).astype(BF16)
    for j in range(N_FF_CHUNKS):
        a = _dot(h, w13_ref[:, j * FF_CHUNK:(j + 1) * FF_CHUNK])
        g = _dot(h, w13_ref[:, D_FF + j * FF_CHUNK:D_FF + (j + 1) * FF_CHUNK])
        u_ref[:, j * FF_CHUNK:(j + 1) * FF_CHUNK] = ((g * jax.nn.sigmoid(g)) * a).astype(BF16)
    y = _dot(u_ref[...], w2_ref[...])
    out = x + (0.5 * gate) * y
    if final:
        out = _rms(out) * gf_ref[...]
    o_ref[0] = out


def _ffn_half(x, mod, sub, norm_g, w13b, w2b, n_groups, ctx=None, final_g=None):
    final = final_g is not None
    split_ctx = ctx is not None
    gf = final_g if final else norm_g
    blk = ROW_TILE * D_MODEL * 4
    wbytes = 3 * D_FF * D_MODEL * 2
    last_tile = SEQ // ROW_TILE - 1
    if split_ctx:
        x_specs = [
            pl.BlockSpec((1, ROW_TILE, D_MODEL),
                         lambda g, i: (jnp.minimum(g, BATCH - 1), jnp.where(g == CTX_GROUP, last_tile, i), 0)),
            pl.BlockSpec((1, ROW_TILE, D_MODEL), lambda g, i: (0, jnp.where(g == CTX_GROUP, i, 0), 0)),
        ]
        xs = (x, ctx)
    else:
        x_specs = [pl.BlockSpec((1, ROW_TILE, D_MODEL), lambda g, i: (g, i, 0))]
        xs = (x,)
    return pl.pallas_call(
        functools.partial(_ffn_kernel, final=final, split_ctx=split_ctx),
        grid=(n_groups, SEQ // ROW_TILE),
        in_specs=x_specs + [
            pl.BlockSpec((1, 1, 3 * D_MODEL), lambda g, i: (g, 0, sub)),
            pl.BlockSpec((1, D_MODEL), lambda g, i: (0, 0)),
            pl.BlockSpec((D_MODEL, 2 * D_FF), lambda g, i: (0, 0), pipeline_mode=pl.Buffered(1)),
            pl.BlockSpec((D_FF, D_MODEL), lambda g, i: (0, 0), pipeline_mode=pl.Buffered(1)),
            pl.BlockSpec((1, D_MODEL), lambda g, i: (0, 0)),
        ],
        out_specs=pl.BlockSpec((1, ROW_TILE, D_MODEL), lambda g, i: (g, i, 0)),
        out_shape=jax.ShapeDtypeStruct((n_groups, SEQ, D_MODEL), F32),
        scratch_shapes=[pltpu.VMEM((ROW_TILE, D_FF), BF16)],
        compiler_params=pltpu.CompilerParams(
            dimension_semantics=("arbitrary", "arbitrary"),
            vmem_limit_bytes=_vmem_limit((2 + split_ctx) * blk, wbytes + ROW_TILE * D_FF * 2, 3 * blk)),
        name="ffn_half_final" if final else ("ffn_half_first" if split_ctx else "ffn_half"),
    )(*xs, mod, norm_g.reshape(1, D_MODEL), w13b, w2b, gf.reshape(1, D_MODEL))


def _norm_rope(x, ones2, gain, cos, sin, bit16):
    sq = x * x
    hi = sq.astype(BF16)
    lo = (sq - hi.astype(F32)).astype(BF16)
    ssum = _dot(jnp.concatenate([hi, lo], axis=1), ones2)
    y = (x * lax.rsqrt(ssum * (1.0 / HEAD_DIM) + EPS)) * gain
    quarter = HEAD_DIM // 4
    partner = jnp.where(bit16, pltpu.roll(y, quarter, 1), pltpu.roll(y, V7X_LANES - quarter, 1))
    return y * cos + partner * sin


def _proj_kernel(x_ref, mod_ref, g_ref, w_ref, ones_ref, gq_ref, gk_ref, cos_ref, sin_ref,
                 qa_ref, ka_ref, va_ref, qb_ref, kb_ref, vb_ref):
    x = x_ref[0]
    m = mod_ref[0]
    shift, scale = m[:, :D_MODEL], m[:, D_MODEL:2 * D_MODEL]
    h = ((_rms(x) * g_ref[...]) * (1.0 + scale) + shift).astype(BF16)
    qb0 = 3 * NA_WIDTH
    kb0 = qb0 + GQA_WIDTH
    cos = cos_ref[0]
    sin = sin_ref[0]
    ones2 = ones_ref[...]
    lane = lax.broadcasted_iota(jnp.int32, (ROW_TILE, V7X_LANES), 1)
    bit16 = (lane & (HEAD_DIM // 4)) != 0
    gq = gq_ref[...] * Q_SCALE
    gk = gk_ref[...]
    q = _dot(h, w_ref[:, qb0:kb0])
    for c in range(GQA_WIDTH // V7X_LANES):
        sl = slice(c * V7X_LANES, (c + 1) * V7X_LANES)
        qb_ref[0, :, sl] = _norm_rope(q[:, sl], ones2, gq, cos, sin, bit16).astype(BF16)
    kv = _dot(h, w_ref[:, kb0:kb0 + 2 * GQA_KV_WIDTH])
    kb_ref[0] = _norm_rope(kv[:, :GQA_KV_WIDTH], ones2, gk, cos, sin, bit16).astype(BF16)
    vb_ref[0] = kv[:, GQA_KV_WIDTH:].astype(BF16)
    qa_ref[0] = (_dot(h, w_ref[:, 0:NA_WIDTH]) * Q_SCALE).astype(BF16)
    ka_ref[0] = _dot(h, w_ref[:, NA_WIDTH:2 * NA_WIDTH]).astype(BF16)
    va_ref[0] = _dot(h, w_ref[:, 2 * NA_WIDTH:3 * NA_WIDTH]).astype(BF16)


def _in_projection(x, mod, norm_g, w_in_b, ones2, gq, gk, cos_t, sin_t):
    blk_in = ROW_TILE * D_MODEL * 4
    blk_out = ROW_TILE * IN_WIDTH * 2
    row = lambda g, i: (g, i, 0)
    tab = lambda g, i: (jnp.where(g == CTX_GROUP, 1, 0), i, 0)
    const2 = lambda g, i: (0, 0)
    shapes = [jax.ShapeDtypeStruct((N_GROUPS, SEQ, w), BF16)
              for w in (NA_WIDTH, NA_WIDTH, NA_WIDTH, GQA_WIDTH, GQA_KV_WIDTH, GQA_KV_WIDTH)]
    return pl.pallas_call(
        _proj_kernel,
        grid=(N_GROUPS, SEQ // ROW_TILE),
        in_specs=[
            pl.BlockSpec((1, ROW_TILE, D_MODEL), row),
            pl.BlockSpec((1, 1, 3 * D_MODEL), lambda g, i: (g, 0, 1)),
            pl.BlockSpec((1, D_MODEL), const2),
            pl.BlockSpec((D_MODEL, IN_WIDTH), const2, pipeline_mode=pl.Buffered(1)),
            pl.BlockSpec((2 * V7X_LANES, V7X_LANES), const2),
            pl.BlockSpec((1, V7X_LANES), const2),
            pl.BlockSpec((1, V7X_LANES), const2),
            pl.BlockSpec((1, ROW_TILE, V7X_LANES), tab),
            pl.BlockSpec((1, ROW_TILE, V7X_LANES), tab),
        ],
        out_specs=[pl.BlockSpec((1, ROW_TILE, s.shape[-1]), row) for s in shapes],
        out_shape=shapes,
        compiler_params=pltpu.CompilerParams(
            dimension_semantics=("parallel", "parallel"),
            vmem_limit_bytes=_vmem_limit(blk_in + blk_out, D_MODEL * IN_WIDTH * 2, 3 * blk_in)),
        name="in_projection",
    )(x, mod, norm_g.reshape(1, D_MODEL), w_in_b, ones2, gq, gk, cos_t, sin_t)


def _with_ones(v, hh):
    low = _low_lanes(v.shape[0])
    one = jnp.ones_like(v)
    return jnp.where(low, v, one) if hh == 0 else jnp.where(low, one, v)


def _normalise(o):
    return o / pltpu.roll(o, HEAD_DIM, 1)


def _softmax_pv(s, v, hh):
    e = jnp.exp2(s - s.max(axis=-1, keepdims=True)).astype(BF16)
    return _normalise(_dot(e, _with_ones(v, hh)))


def _low_lanes(n):
    return lax.broadcasted_iota(jnp.int32, (n, V7X_LANES), 1) < HEAD_DIM


def _mask_head(q, hh):
    low = _low_lanes(q.shape[0])
    zero = jnp.zeros_like(q)
    return jnp.where(low, q, zero) if hh == 0 else jnp.where(low, zero, q)


def _two_stage_steps(j, init, step):
    @pl.when(j == 0)
    def _():
        init(1)

    @pl.when(j % 2 == 0)
    def _():
        step(0, 1)

    @pl.when(j % 2 == 1)
    def _():
        step(1, 0)


def _na_first_key(i):
    first_key_row = jnp.clip(NA_Q_ROWS * i - NA_WIN_H // 2, 0, GRID_H - NA_K_ROWS)
    return pl.multiple_of(first_key_row * GRID_W, GRID_W)


def _na_kernel(q_ref, k_ref, kc_ref, bias_ref, v_ref, vc_ref, o_ref, e_ref, ec_ref):
    j = pl.program_id(0)
    k0_a = _na_first_key(jnp.minimum(j, NA_TILES - 1) % NA_BLOCKS)
    k0_b = _na_first_key(jnp.maximum(j - 1, 0) % NA_BLOCKS)

    def init(slot):
        e_ref[slot] = jnp.ones(e_ref.shape[1:], BF16)
        ec_ref[slot] = jnp.ones(ec_ref.shape[1:], BF16)

    def store_exp(slot, h, s_loc, s_ctx, m):
        e_ref[slot, h] = jnp.exp2(s_loc - m).astype(BF16)
        ec_ref[slot, h] = jnp.exp2(s_ctx - m).astype(BF16)

    def step(slot_a, slot_b):
        low = _low_lanes(NA_Q_TILE)
        pending = None
        o_pair = []
        for h in range(NA_HEADS):
            c, hh = divmod(h, 2)
            sl = slice(c * V7X_LANES, (c + 1) * V7X_LANES)
            qm = _mask_head(q_ref[0, :, sl], hh)
            s_loc = _dot_nt(qm, k_ref[0, pl.ds(k0_a, NA_K_TILE), sl]) + bias_ref[0, h]
            s_ctx = _dot_nt(qm, kc_ref[0, :, sl])
            m = jnp.maximum(s_loc.max(axis=-1, keepdims=True), s_ctx.max(axis=-1, keepdims=True))
            if pending is not None:
                store_exp(slot_a, *pending)
            pending = (h, s_loc, s_ctx, m)
            vw = _with_ones(v_ref[0, pl.ds(k0_b, NA_K_TILE), sl], hh)
            vc = _with_ones(vc_ref[0, :, sl], hh)
            o_pair.append(_normalise(_dot(e_ref[slot_b, h], vw) + _dot(ec_ref[slot_b, h], vc)))
            if hh == 1:
                o_ref[0, :, sl] = jnp.where(low, o_pair[0], o_pair[1])
                o_pair = []
        store_exp(slot_a, *pending)

    _two_stage_steps(j, init, step)


def _na_bias_table(rpb):
    period = 2 * GRID_W
    lead = rpb.shape[:2]
    ring = jnp.concatenate(
        [rpb[..., NA_WIN_W - 1:], jnp.zeros(lead + (RPB_H, period - RPB_W), F32),
         rpb[..., :NA_WIN_W - 1]], axis=-1)
    flat = jnp.tile(ring, (1, 1, 1, GRID_W))[..., :GRID_W * (period - 1)]
    toe = flat.reshape(lead + (RPB_H, GRID_W, period - 1))[..., :GRID_W]
    qc = np.arange(GRID_W)[:, None]
    kc = np.arange(GRID_W)[None, :]
    cs = np.clip(qc - NA_WIN_W // 2, 0, GRID_W - NA_WIN_W)
    in_win = (kc >= cs) & (kc < cs + NA_WIN_W)
    masked = jnp.full(lead + (1, GRID_W, GRID_W), NEG_INF, F32)
    blocks = jnp.concatenate([jnp.where(in_win, toe, NEG_INF), masked], axis=2)
    picks = np.full((3, NA_Q_ROWS, NA_K_ROWS), RPB_H, np.int32)
    for v, i in enumerate((0, 1, NA_BLOCKS - 1)):
        first_key_row = min(max(NA_Q_ROWS * i - NA_WIN_H // 2, 0), GRID_H - NA_K_ROWS)
        for a in range(NA_Q_ROWS):
            qr = NA_Q_ROWS * i + a
            rs = min(max(qr - NA_WIN_H // 2, 0), GRID_H - NA_WIN_H)
            for n in range(NA_K_ROWS):
                kr = first_key_row + n
                if rs <= kr < rs + NA_WIN_H:
                    picks[v, a, n] = kr - qr + NA_WIN_H - 1
    t = blocks[:, :, picks]
    t = jnp.transpose(t, (0, 2, 1, 3, 5, 4, 6))
    return t.reshape(rpb.shape[0], 3, NA_HEADS, NA_Q_TILE, NA_K_TILE)


def _na_attention(qa, ka, va, bias):
    tile_a = lambda j: jnp.minimum(j, NA_TILES - 1)
    tile_b = lambda j: jnp.maximum(j - 1, 0)

    def variant(j):
        i = tile_a(j) % NA_BLOCKS
        return (jnp.where(i == 0, 0, jnp.where(i == NA_BLOCKS - 1, 2, 1)), 0, 0, 0)

    blk = (NA_Q_TILE * NA_WIDTH * (2 + 4) + 2 * (SEQ + CTX_LEN) * NA_WIDTH * 2
           + NA_HEADS * NA_Q_TILE * NA_K_TILE * 4)
    scratch = 2 * NA_HEADS * NA_Q_TILE * (NA_K_TILE + CTX_LEN) * 2
    scores = NA_HEADS * NA_Q_TILE * (NA_K_TILE + CTX_LEN) * 4
    return pl.pallas_call(
        _na_kernel,
        grid=(NA_TILES + 1,),
        in_specs=[
            pl.BlockSpec((1, NA_Q_TILE, NA_WIDTH), lambda j: (tile_a(j) // NA_BLOCKS, tile_a(j) % NA_BLOCKS, 0)),
            pl.BlockSpec((1, SEQ, NA_WIDTH), lambda j: (tile_a(j) // NA_BLOCKS, 0, 0)),
            pl.BlockSpec((1, CTX_LEN, NA_WIDTH), lambda j: (CTX_GROUP, tile_a(j) // NA_BLOCKS, 0)),
            pl.BlockSpec((1, NA_HEADS, NA_Q_TILE, NA_K_TILE), variant),
            pl.BlockSpec((1, SEQ, NA_WIDTH), lambda j: (tile_b(j) // NA_BLOCKS, 0, 0)),
            pl.BlockSpec((1, CTX_LEN, NA_WIDTH), lambda j: (CTX_GROUP, tile_b(j) // NA_BLOCKS, 0)),
        ],
        out_specs=pl.BlockSpec((1, NA_Q_TILE, NA_WIDTH),
                               lambda j: (tile_b(j) // NA_BLOCKS, tile_b(j) % NA_BLOCKS, 0)),
        out_shape=jax.ShapeDtypeStruct((BATCH, SEQ, NA_WIDTH), F32),
        scratch_shapes=[pltpu.VMEM((2, NA_HEADS, NA_Q_TILE, NA_K_TILE), BF16),
                        pltpu.VMEM((2, NA_HEADS, NA_Q_TILE, CTX_LEN), BF16)],
        compiler_params=pltpu.CompilerParams(
            dimension_semantics=("arbitrary",),
            vmem_limit_bytes=_vmem_limit(blk, scratch, scores)),
        name="neighbourhood_attention",
    )(qa, ka, ka, bias, va, va)


def _gqa_stack(q, kvh):
    return jnp.concatenate(
        [_mask_head(q[:, c * V7X_LANES:(c + 1) * V7X_LANES], kvh) for c in range(GQA_GROUP)], axis=0)


def _gqa_unstack(o_kv0, o_kv1, n_q):
    low = _low_lanes(n_q)
    return [jnp.where(low, o_kv0[c * n_q:(c + 1) * n_q], o_kv1[c * n_q:(c + 1) * n_q])
            for c in range(GQA_GROUP)]


def _gqa_kernel(q_ref, kt_ref, kct_ref, v_ref, vc_ref, o_ref, e_ref):
    j = pl.program_id(0)
    chains = [(t, kvh) for t in range(GQA_Q_TILE // GQA_SUB_TILE) for kvh in range(GQA_KV_HEADS)]
    n_lat = SEQ // GQA_KEY_CHUNK
    n_chunks = n_lat + CTX_LEN // GQA_KEY_CHUNK

    def init(slot):
        e_ref[slot] = jnp.ones(e_ref.shape[1:], BF16)

    def step(slot_a, slot_b):
        v_ones = [(_with_ones(v_ref[0], kvh), _with_ones(vc_ref[0], kvh)) for kvh in range(GQA_KV_HEADS)]

        def chunk(ref_lat, ref_ctx, c, axis):
            src, cc = (ref_lat, c) if c < n_lat else (ref_ctx, c - n_lat)
            keys = slice(cc * GQA_KEY_CHUNK, (cc + 1) * GQA_KEY_CHUNK)
            return src[keys] if axis == 0 else src[0, :, keys]

        n_slots = len(chains) + 1
        n_units = len(chains) * n_chunks
        unit_at = {u: u for u in range(n_units)}
        scores, row_max, acc, outs = {}, {}, {}, {}
        for i in range(n_slots):
            live = i < len(chains)
            if live:
                t, kvh = chains[i]
                stack = _gqa_stack(q_ref[0, t * GQA_SUB_TILE:(t + 1) * GQA_SUB_TILE, :], kvh)
                scores[i], lane_max = [], None
            for c in range(n_chunks):
                keys = slice(c * GQA_KEY_CHUNK, (c + 1) * GQA_KEY_CHUNK)
                if live:
                    s = _dot(stack, chunk(kt_ref, kct_ref, c, 1))
                    scores[i].append(s)
                    for w in range(GQA_KEY_CHUNK // V7X_LANES):
                        part = s[:, w * V7X_LANES:(w + 1) * V7X_LANES]
                        lane_max = part if lane_max is None else jnp.maximum(lane_max, part)
                if i >= 1:
                    e_ref[slot_a, i - 1, :, keys] = jnp.exp2(scores[i - 1][c] - row_max[i - 1]).astype(BF16)
                u = unit_at.get(i * n_chunks + c)
                if u is not None:
                    b, cb = divmod(u, n_chunks)
                    tb, kvb = chains[b]
                    bkeys = slice(cb * GQA_KEY_CHUNK, (cb + 1) * GQA_KEY_CHUNK)
                    part = _dot(e_ref[slot_b, b, :, bkeys], chunk(v_ones[kvb][0], v_ones[kvb][1], cb, 0))
                    acc[b] = part if cb == 0 else acc[b] + part
                    if cb == n_chunks - 1:
                        outs[kvb] = _normalise(acc.pop(b))
                        if kvb == GQA_KV_HEADS - 1:
                            rows = slice(tb * GQA_SUB_TILE, (tb + 1) * GQA_SUB_TILE)
                            for col, oc in enumerate(_gqa_unstack(outs[0], outs[1], GQA_SUB_TILE)):
                                o_ref[0, rows, col * V7X_LANES:(col + 1) * V7X_LANES] = oc
            if i >= 1:
                del scores[i - 1]
            if live:
                row_max[i] = lane_max.max(axis=-1, keepdims=True)

    _two_stage_steps(j, init, step)


def _gqa_attention(qb, kbt, vb):
    tile_a = lambda j: jnp.minimum(j, GQA_TILES - 1)
    tile_b = lambda j: jnp.maximum(j - 1, 0)
    per_b = SEQ // GQA_Q_TILE
    chains = GQA_KV_HEADS * GQA_Q_TILE // GQA_SUB_TILE
    chain_rows = GQA_GROUP * GQA_SUB_TILE
    blk = (GQA_Q_TILE * GQA_WIDTH * (2 + 4) + 2 * (SEQ + CTX_LEN) * V7X_LANES * 2)
    scratch = 2 * chains * chain_rows * (SEQ + CTX_LEN) * 2
    scores = chain_rows * (SEQ + CTX_LEN) * 4
    return pl.pallas_call(
        _gqa_kernel,
        grid=(GQA_TILES + 1,),
        in_specs=[
            pl.BlockSpec((1, GQA_Q_TILE, GQA_WIDTH), lambda j: (tile_a(j) // per_b, tile_a(j) % per_b, 0)),
            pl.BlockSpec((1, GQA_KV_WIDTH, SEQ), lambda j: (tile_a(j) // per_b, 0, 0)),
            pl.BlockSpec((1, GQA_KV_WIDTH, CTX_LEN), lambda j: (CTX_GROUP, 0, tile_a(j) // per_b)),
            pl.BlockSpec((1, SEQ, GQA_KV_WIDTH), lambda j: (tile_b(j) // per_b, 0, 0)),
            pl.BlockSpec((1, CTX_LEN, GQA_KV_WIDTH), lambda j: (CTX_GROUP, tile_b(j) // per_b, 0)),
        ],
        out_specs=pl.BlockSpec((1, GQA_Q_TILE, GQA_WIDTH), lambda j: (tile_b(j) // per_b, tile_b(j) % per_b, 0)),
        out_shape=jax.ShapeDtypeStruct((BATCH, SEQ, GQA_WIDTH), F32),
        scratch_shapes=[pltpu.VMEM((2, chains, chain_rows, SEQ + CTX_LEN), BF16)],
        compiler_params=pltpu.CompilerParams(
            dimension_semantics=("arbitrary",),
            vmem_limit_bytes=_vmem_limit(blk, scratch, 4 * scores)),
        name="gqa_attention",
    )(qb, kbt, kbt, vb, vb)


def _ctx_kernel(qa_ref, ka_ref, va_ref, qb_ref, kbt_ref, vb_ref, oa_ref, ob_ref):
    low = _low_lanes(CTX_LEN)
    for c in range(NA_WIDTH // V7X_LANES):
        sl = slice(c * V7X_LANES, (c + 1) * V7X_LANES)
        q, k, v = qa_ref[0, :, sl], ka_ref[0, :, sl], va_ref[0, :, sl]
        o = [_softmax_pv(_dot_nt(_mask_head(q, hh), k), v, hh) for hh in range(2)]
        oa_ref[0, :, sl] = jnp.where(low, o[0], o[1])
    q = qb_ref[0]
    o = [_softmax_pv(_dot(_gqa_stack(q, kvh), kbt_ref[0]), vb_ref[0], kvh) for kvh in range(GQA_KV_HEADS)]
    for c, oc in enumerate(_gqa_unstack(o[0], o[1], CTX_LEN)):
        ob_ref[0, :, c * V7X_LANES:(c + 1) * V7X_LANES] = oc


def _ctx_attention(qa, ka, va, qb, kbt, vb):
    row = lambda b: (CTX_GROUP, b, 0)
    wide = pl.BlockSpec((1, CTX_LEN, NA_WIDTH), row)
    narrow = pl.BlockSpec((1, CTX_LEN, GQA_KV_WIDTH), row)
    narrow_t = pl.BlockSpec((1, GQA_KV_WIDTH, CTX_LEN), lambda b: (CTX_GROUP, 0, b))
    out = pl.BlockSpec((1, CTX_LEN, NA_WIDTH), lambda b: (0, b, 0))
    shape = jax.ShapeDtypeStruct((1, SEQ, NA_WIDTH), F32)
    blk = CTX_LEN * (4 * NA_WIDTH * 2 + 2 * GQA_KV_WIDTH * 2 + 2 * NA_WIDTH * 4)
    return pl.pallas_call(
        _ctx_kernel,
        grid=(BATCH,),
        in_specs=[wide, wide, wide, wide, narrow_t, narrow],
        out_specs=[out, out],
        out_shape=[shape, shape],
        compiler_params=pltpu.CompilerParams(
            dimension_semantics=("parallel",),
            vmem_limit_bytes=_vmem_limit(blk, 0, 16 * GQA_GROUP * CTX_LEN * CTX_LEN * 4)),
        name="context_attention",
    )(qa, ka, va, qb, kbt, vb)


def _outproj_kernel(*refs, split_ctx):
    if split_ctx:
        x_ref, oa_ref, ob_ref, oac_ref, obc_ref, mod_ref, ga_ref, gb_ref, w_ref, o_ref = refs
        is_ctx = pl.program_id(0) == CTX_GROUP
        oa = jnp.where(is_ctx, oac_ref[0], oa_ref[0])
        ob = jnp.where(is_ctx, obc_ref[0], ob_ref[0])
    else:
        x_ref, oa_ref, ob_ref, mod_ref, ga_ref, gb_ref, w_ref, o_ref = refs
        oa, ob = oa_ref[0], ob_ref[0]
    gate = mod_ref[0][:, 2 * D_MODEL:]
    ya = (_rms(oa) * ga_ref[...]).astype(BF16)
    yb = (_rms(ob) * gb_ref[...]).astype(BF16)
    y = _dot(ya, w_ref[:NA_WIDTH, :]) + _dot(yb, w_ref[NA_WIDTH:, :])
    o_ref[0] = x_ref[0] + gate * y


def _out_projection(x, oa, ob, ctx_o, mod, ga, gb, w_out_b):
    split_ctx = ctx_o is not None
    n_groups = N_GROUPS if split_ctx else BATCH
    row = lambda g, i: (g, i, 0)
    const2 = lambda g, i: (0, 0)
    last_tile = SEQ // ROW_TILE - 1
    lat = lambda g, i: (jnp.minimum(g, BATCH - 1), jnp.where(g == CTX_GROUP, last_tile, i), 0)
    cxt = lambda g, i: (0, jnp.where(g == CTX_GROUP, i, 0), 0)
    att = lambda imap: pl.BlockSpec((1, ROW_TILE, NA_WIDTH), imap)
    att_specs = [att(lat), att(lat), att(cxt), att(cxt)] if split_ctx else [att(row), att(row)]
    att_args = (oa, ob) + (tuple(ctx_o) if split_ctx else ())
    blk = ROW_TILE * (2 * D_MODEL + (2 + 2 * split_ctx) * NA_WIDTH) * 4
    return pl.pallas_call(
        functools.partial(_outproj_kernel, split_ctx=split_ctx),
        grid=(n_groups, SEQ // ROW_TILE),
        in_specs=[pl.BlockSpec((1, ROW_TILE, D_MODEL), row)] + att_specs + [
            pl.BlockSpec((1, 1, 3 * D_MODEL), lambda g, i: (g, 0, 1)),
            pl.BlockSpec((1, NA_WIDTH), const2),
            pl.BlockSpec((1, GQA_WIDTH), const2),
            pl.BlockSpec((NA_WIDTH + GQA_WIDTH, D_MODEL), const2, pipeline_mode=pl.Buffered(1)),
        ],
        out_specs=pl.BlockSpec((1, ROW_TILE, D_MODEL), row),
        out_shape=jax.ShapeDtypeStruct((n_groups, SEQ, D_MODEL), F32),
        compiler_params=pltpu.CompilerParams(
            dimension_semantics=("arbitrary", "arbitrary"),
            vmem_limit_bytes=_vmem_limit(blk, (NA_WIDTH + GQA_WIDTH) * D_MODEL * 2, blk)),
        name="out_projection_ctx" if split_ctx else "out_projection",
    )(x, *att_args, mod, ga.reshape(1, NA_WIDTH), gb.reshape(1, GQA_WIDTH), w_out_b)


def _gqa_head_permutation():
    p = np.arange(GQA_WIDTH)
    col, half, d = p // V7X_LANES, (p % V7X_LANES) // HEAD_DIM, p % HEAD_DIM
    return HEAD_DIM * (col + GQA_GROUP * half) + d


def _rope_tables():
    t = jnp.arange(SEQ, dtype=jnp.int32)
    row = (t // GRID_W).astype(F32)
    col = (t % GRID_W).astype(F32)
    axis_dim = HEAD_DIM // 2
    inv_freq = ROPE_THETA ** (-jnp.arange(0, axis_dim, 2, dtype=F32) / axis_dim)
    ang_r = row[:, None] * inv_freq[None, :]
    ang_c = col[:, None] * inv_freq[None, :]
    cos_r, sin_r, cos_c, sin_c = jnp.cos(ang_r), jnp.sin(ang_r), jnp.cos(ang_c), jnp.sin(ang_c)
    cos_h = jnp.concatenate([cos_r, cos_r, cos_c, cos_c], axis=-1)
    sin_h = jnp.concatenate([-sin_r, sin_r, -sin_c, sin_c], axis=-1)
    heads = V7X_LANES // HEAD_DIM
    cos2 = jnp.tile(cos_h, (1, heads))
    sin2 = jnp.tile(sin_h, (1, heads))
    return (jnp.stack([cos2, jnp.ones_like(cos2)]), jnp.stack([sin2, jnp.zeros_like(sin2)]))


def kernel(x, c, ctx, c_ctx, w_mod, b_mod, norm_ffn1, ffn1_w13, ffn1_w2, norm_mix, w_in, na_rpb,
           gqa_q_norm, gqa_k_norm, out_norm_a, out_norm_b, w_out, norm_ffn2, ffn2_w13, ffn2_w2, norm_f):
    perm = _gqa_head_permutation()
    cos_t, sin_t = _rope_tables()
    head_of_lane = np.arange(V7X_LANES) // HEAD_DIM
    ones_blk = np.asarray(head_of_lane[:, None] == head_of_lane[None, :], np.float32)
    ones2 = jnp.asarray(np.concatenate([ones_blk, ones_blk], axis=0), BF16)
    heads_per_col = V7X_LANES // HEAD_DIM
    bias_all = _na_bias_table(na_rpb) * LOG2_E

    cond = jnp.zeros((COND_ROWS, D_MODEL), F32).at[:BATCH].set(c).at[CTX_GROUP].set(c_ctx)
    mod_all = _modulation(cond, w_mod, b_mod)

    xs = x
    for l in range(DEPTH):
        last = l == DEPTH - 1
        mod = mod_all[l].reshape(COND_ROWS, 1, N_MOD * D_MODEL)
        qb0 = 3 * NA_WIDTH
        w_in_l = w_in[l]
        w_in_b = jnp.concatenate(
            [w_in_l[:, :qb0], w_in_l[:, qb0:qb0 + GQA_WIDTH][:, perm], w_in_l[:, qb0 + GQA_WIDTH:]],
            axis=1).astype(BF16)
        w_out_l = w_out[l]
        w_out_b = jnp.concatenate([w_out_l[:NA_WIDTH], w_out_l[NA_WIDTH:][perm]], axis=0).astype(BF16)
        gq = jnp.tile(gqa_q_norm[l], heads_per_col).reshape(1, V7X_LANES)
        gk = jnp.tile(gqa_k_norm[l], heads_per_col).reshape(1, V7X_LANES)

        xs = _ffn_half(xs, mod, 0, norm_ffn1[l], ffn1_w13[l].astype(BF16), ffn1_w2[l].astype(BF16),
                       N_GROUPS, ctx=ctx.reshape(1, SEQ, D_MODEL) if l == 0 else None)
        qa, ka, va, qb, kb, vb = _in_projection(xs, mod, norm_mix[l], w_in_b, ones2, gq, gk,
                                                cos_t, sin_t)
        kbt = jnp.swapaxes(kb, 1, 2)
        oa = _na_attention(qa, ka, va, bias_all[l])
        ob = _gqa_attention(qb, kbt, vb)
        ctx_o = None if last else _ctx_attention(qa, ka, va, qb, kbt, vb)
        xs = _out_projection(xs, oa, ob, ctx_o, mod, out_norm_a[l], out_norm_b[l][perm], w_out_b)
        xs = _ffn_half(xs, mod, 2, norm_ffn2[l], ffn2_w13[l].astype(BF16), ffn2_w2[l].astype(BF16),
                       BATCH if last else N_GROUPS, final_g=norm_f if last else None)
    return xs
```

```python
import functools

import numpy as np
import jax
import jax.numpy as jnp
from jax import lax
from jax.experimental import pallas as pl
from jax.experimental.pallas import tpu as pltpu

F32 = jnp.float32
BF16 = jnp.bfloat16

D_MODEL = 1024
BATCH = 8
SEQ = 2048
DEPTH = 2
GRID_W = 64
GRID_H = SEQ // GRID_W
CTX_LEN = 256
HEAD_DIM = 64
NA_HEADS = 8
GQA_HEADS = 8
GQA_KV_HEADS = 2
GQA_GROUP = GQA_HEADS // GQA_KV_HEADS
NA_WIDTH = NA_HEADS * HEAD_DIM
GQA_WIDTH = GQA_HEADS * HEAD_DIM
GQA_KV_WIDTH = GQA_KV_HEADS * HEAD_DIM
IN_WIDTH = 3 * NA_WIDTH + GQA_WIDTH + 2 * GQA_KV_WIDTH
NA_WIN_H = 8
NA_WIN_W = 16
RPB_H = 2 * NA_WIN_H - 1
RPB_W = 2 * NA_WIN_W - 1
D_FF = 2816
N_MOD = 9
ROPE_THETA = 10000.0
EPS = 1e-6
NEG_INF = -1e30
LOG2_E = 1.4426950408889634
Q_SCALE = LOG2_E * HEAD_DIM ** -0.5

N_GROUPS = BATCH + 1
CTX_GROUP = BATCH
COND_ROWS = 16

V7X_LANES = 128
V7X_VMEM_BYTES = 64 * 1024 * 1024
V7X_VMEM_REQUEST_CAP = 56 * 1024 * 1024

ROW_TILE = 512
FF_CHUNK = 256
N_FF_CHUNKS = D_FF // FF_CHUNK
MOD_TILE = 1152
NA_Q_ROWS = 4
NA_Q_TILE = NA_Q_ROWS * GRID_W
NA_K_ROWS = NA_Q_ROWS + NA_WIN_H - 1
NA_K_TILE = NA_K_ROWS * GRID_W
NA_BLOCKS = GRID_H // NA_Q_ROWS
NA_TILES = BATCH * NA_BLOCKS
GQA_Q_TILE = 256
GQA_SUB_TILE = 64
GQA_TILES = BATCH * (SEQ // GQA_Q_TILE)
GQA_KEY_CHUNK = 256

assert BATCH * CTX_LEN == SEQ
assert D_FF % FF_CHUNK == 0 and SEQ % ROW_TILE == 0 and (N_MOD * D_MODEL) % MOD_TILE == 0
assert GRID_H % NA_Q_ROWS == 0 and SEQ % GQA_Q_TILE == 0 and GQA_Q_TILE % GQA_SUB_TILE == 0
assert SEQ % GQA_KEY_CHUNK == 0 and CTX_LEN % GQA_KEY_CHUNK == 0


def _vmem_limit(block_bytes, scratch_bytes, temp_bytes):
    need = 2 * block_bytes + scratch_bytes + temp_bytes
    return int(min(max(need, 16 * 1024 * 1024), V7X_VMEM_REQUEST_CAP))


def _rms(x):
    return x * lax.rsqrt(jnp.mean(x * x, axis=-1, keepdims=True) + EPS)


def _dot(a, b):
    return jnp.dot(a, b, preferred_element_type=F32)


def _dot_nt(a, b):
    return lax.dot_general(a, b, (((1,), (1,)), ((), ())), preferred_element_type=F32)


def _mod_kernel(c_ref, w_ref, b_ref, o_ref):
    c = c_ref[...]
    s = (c * jax.nn.sigmoid(c)).astype(BF16)
    o_ref[0] = _dot(s, w_ref[0].astype(BF16)) + b_ref[0]


def _modulation(cond, w_mod, b_mod):
    n = N_MOD * D_MODEL
    return pl.pallas_call(
        _mod_kernel,
        grid=(DEPTH, n // MOD_TILE),
        in_specs=[
            pl.BlockSpec((COND_ROWS, D_MODEL), lambda l, j: (0, 0)),
            pl.BlockSpec((1, D_MODEL, MOD_TILE), lambda l, j: (l, 0, j)),
            pl.BlockSpec((1, 1, MOD_TILE), lambda l, j: (l, 0, j)),
        ],
        out_specs=pl.BlockSpec((1, COND_ROWS, MOD_TILE), lambda l, j: (l, 0, j)),
        out_shape=jax.ShapeDtypeStruct((DEPTH, COND_ROWS, n), F32),
        compiler_params=pltpu.CompilerParams(
            dimension_semantics=("parallel", "parallel"),
            vmem_limit_bytes=_vmem_limit(D_MODEL * MOD_TILE * 4, 0, D_MODEL * MOD_TILE * 2)),
        name="adaln_modulation",
    )(cond, w_mod, b_mod.reshape(DEPTH, 1, n))


CAST_BLOCK_BYTES = 6 * 1024 * 1024


def _cast_kernel(w_ref, o_ref):
    o_ref[...] = w_ref[...].astype(BF16)


def _cast_bf16(w):
    layers, rows, cols = w.shape
    tile = rows
    while tile * cols * 4 > CAST_BLOCK_BYTES and tile % 32 == 0:
        tile //= 2
    spec = pl.BlockSpec((1, tile, cols), lambda l, i: (l, i, 0))
    return pl.pallas_call(
        _cast_kernel,
        grid=(layers, rows // tile),
        in_specs=[spec],
        out_specs=spec,
        out_shape=jax.ShapeDtypeStruct(w.shape, BF16),
        compiler_params=pltpu.CompilerParams(
            dimension_semantics=("parallel", "parallel"),
            vmem_limit_bytes=_vmem_limit(tile * cols * 6, 0, tile * cols * 2)),
        name="cast_bf16",
    )(w)


def _ffn_kernel(*refs, final, split_ctx):
    if split_ctx:
        x_ref, ctx_ref, mod_ref, g_ref, w13_ref, w2_ref, gf_ref, o_ref, u_ref = refs
        x = jnp.where(pl.program_id(0) == CTX_GROUP, ctx_ref[0], x_ref[0])
    else:
        x_ref, mod_ref, g_ref, w13_ref, w2_ref, gf_ref, o_ref, u_ref = refs
        x = x_ref[0]
    m = mod_ref[0]
    shift, scale, gate = m[:, :D_MODEL], m[:, D_MODEL:2 * D_MODEL], m[:, 2 * D_MODEL:]
    h = ((_rms(x) * g_ref[...]) * (1.0 + scale) + shift).astype(BF16)
    for j in range(N_FF_CHUNKS):
        a = _dot(h, w13_ref[0, :, j * FF_CHUNK:(j + 1) * FF_CHUNK])
        g = _dot(h, w13_ref[0, :, D_FF + j * FF_CHUNK:D_FF + (j + 1) * FF_CHUNK])
        u_ref[:, j * FF_CHUNK:(j + 1) * FF_CHUNK] = ((g * jax.nn.sigmoid(g)) * a).astype(BF16)
    y = _dot(u_ref[...], w2_ref[0])
    out = x + (0.5 * gate) * y
    if final:
        out = _rms(out) * gf_ref[...]
    o_ref[0] = out


def _ffn_half(x, mod, sub, norm_g, w13b, w2b, layer, n_groups, ctx=None, final_g=None):
    final = final_g is not None
    split_ctx = ctx is not None
    gf = final_g if final else norm_g
    blk = ROW_TILE * D_MODEL * 4
    wbytes = 3 * D_FF * D_MODEL * 2
    last_tile = SEQ // ROW_TILE - 1
    if split_ctx:
        x_specs = [
            pl.BlockSpec((1, ROW_TILE, D_MODEL),
                         lambda g, i: (jnp.minimum(g, BATCH - 1), jnp.where(g == CTX_GROUP, last_tile, i), 0)),
            pl.BlockSpec((1, ROW_TILE, D_MODEL), lambda g, i: (0, jnp.where(g == CTX_GROUP, i, 0), 0)),
        ]
        xs = (x, ctx)
    else:
        x_specs = [pl.BlockSpec((1, ROW_TILE, D_MODEL), lambda g, i: (g, i, 0))]
        xs = (x,)
    return pl.pallas_call(
        functools.partial(_ffn_kernel, final=final, split_ctx=split_ctx),
        grid=(n_groups, SEQ // ROW_TILE),
        in_specs=x_specs + [
            pl.BlockSpec((1, 1, 3 * D_MODEL), lambda g, i: (g, 0, sub)),
            pl.BlockSpec((1, D_MODEL), lambda g, i: (0, 0)),
            pl.BlockSpec((1, D_MODEL, 2 * D_FF), lambda g, i: (layer, 0, 0), pipeline_mode=pl.Buffered(1)),
            pl.BlockSpec((1, D_FF, D_MODEL), lambda g, i: (layer, 0, 0), pipeline_mode=pl.Buffered(1)),
            pl.BlockSpec((1, D_MODEL), lambda g, i: (0, 0)),
        ],
        out_specs=pl.BlockSpec((1, ROW_TILE, D_MODEL), lambda g, i: (g, i, 0)),
        out_shape=jax.ShapeDtypeStruct((n_groups, SEQ, D_MODEL), F32),
        scratch_shapes=[pltpu.VMEM((ROW_TILE, D_FF), BF16)],
        compiler_params=pltpu.CompilerParams(
            dimension_semantics=("arbitrary", "arbitrary"),
            vmem_limit_bytes=_vmem_limit((2 + split_ctx) * blk, wbytes + ROW_TILE * D_FF * 2, 3 * blk)),
        name="ffn_half_final" if final else ("ffn_half_first" if split_ctx else "ffn_half"),
    )(*xs, mod, norm_g.reshape(1, D_MODEL), w13b, w2b, gf.reshape(1, D_MODEL))


def _norm_rope(x, ones2, gain, cos, sin, bit16):
    sq = x * x
    hi = sq.astype(BF16)
    lo = (sq - hi.astype(F32)).astype(BF16)
    ssum = _dot(jnp.concatenate([hi, lo], axis=1), ones2)
    y = (x * lax.rsqrt(ssum * (1.0 / HEAD_DIM) + EPS)) * gain
    quarter = HEAD_DIM // 4
    partner = jnp.where(bit16, pltpu.roll(y, quarter, 1), pltpu.roll(y, V7X_LANES - quarter, 1))
    return y * cos + partner * sin


def _proj_kernel(x_ref, mod_ref, g_ref, w_ref, ones_ref, gq_ref, gk_ref, cos_ref, sin_ref,
                 qa_ref, ka_ref, va_ref, qb_ref, kb_ref, vb_ref):
    x = x_ref[0]
    m = mod_ref[0]
    shift, scale = m[:, :D_MODEL], m[:, D_MODEL:2 * D_MODEL]
    h = ((_rms(x) * g_ref[...]) * (1.0 + scale) + shift).astype(BF16)
    qb0 = 3 * NA_WIDTH
    kb0 = qb0 + GQA_WIDTH
    cos = cos_ref[0]
    sin = sin_ref[0]
    ones2 = ones_ref[...]
    lane = lax.broadcasted_iota(jnp.int32, (ROW_TILE, V7X_LANES), 1)
    bit16 = (lane & (HEAD_DIM // 4)) != 0
    gq = gq_ref[...] * Q_SCALE
    gk = gk_ref[...]
    q = _dot(h, w_ref[:, qb0:kb0])
    for c in range(GQA_WIDTH // V7X_LANES):
        sl = slice(c * V7X_LANES, (c + 1) * V7X_LANES)
        qb_ref[0, :, sl] = _norm_rope(q[:, sl], ones2, gq, cos, sin, bit16).astype(BF16)
    kv = _dot(h, w_ref[:, kb0:kb0 + 2 * GQA_KV_WIDTH])
    kb_ref[0] = _norm_rope(kv[:, :GQA_KV_WIDTH], ones2, gk, cos, sin, bit16).astype(BF16)
    vb_ref[0] = kv[:, GQA_KV_WIDTH:].astype(BF16)
    qa_ref[0] = (_dot(h, w_ref[:, 0:NA_WIDTH]) * Q_SCALE).astype(BF16)
    ka_ref[0] = _dot(h, w_ref[:, NA_WIDTH:2 * NA_WIDTH]).astype(BF16)
    va_ref[0] = _dot(h, w_ref[:, 2 * NA_WIDTH:3 * NA_WIDTH]).astype(BF16)


def _in_projection(x, mod, norm_g, w_in_b, ones2, gq, gk, cos_t, sin_t):
    blk_in = ROW_TILE * D_MODEL * 4
    blk_out = ROW_TILE * IN_WIDTH * 2
    row = lambda g, i: (g, i, 0)
    tab = lambda g, i: (jnp.where(g == CTX_GROUP, 1, 0), i, 0)
    const2 = lambda g, i: (0, 0)
    shapes = [jax.ShapeDtypeStruct((N_GROUPS, SEQ, w), BF16)
              for w in (NA_WIDTH, NA_WIDTH, NA_WIDTH, GQA_WIDTH, GQA_KV_WIDTH, GQA_KV_WIDTH)]
    return pl.pallas_call(
        _proj_kernel,
        grid=(N_GROUPS, SEQ // ROW_TILE),
        in_specs=[
            pl.BlockSpec((1, ROW_TILE, D_MODEL), row),
            pl.BlockSpec((1, 1, 3 * D_MODEL), lambda g, i: (g, 0, 1)),
            pl.BlockSpec((1, D_MODEL), const2),
            pl.BlockSpec((D_MODEL, IN_WIDTH), const2, pipeline_mode=pl.Buffered(1)),
            pl.BlockSpec((2 * V7X_LANES, V7X_LANES), const2),
            pl.BlockSpec((1, V7X_LANES), const2),
            pl.BlockSpec((1, V7X_LANES), const2),
            pl.BlockSpec((1, ROW_TILE, V7X_LANES), tab),
            pl.BlockSpec((1, ROW_TILE, V7X_LANES), tab),
        ],
        out_specs=[pl.BlockSpec((1, ROW_TILE, s.shape[-1]), row) for s in shapes],
        out_shape=shapes,
        compiler_params=pltpu.CompilerParams(
            dimension_semantics=("parallel", "parallel"),
            vmem_limit_bytes=_vmem_limit(blk_in + blk_out, D_MODEL * IN_WIDTH * 2, 3 * blk_in)),
        name="in_projection",
    )(x, mod, norm_g.reshape(1, D_MODEL), w_in_b, ones2, gq, gk, cos_t, sin_t)


def _with_ones(v, hh):
    low = _low_lanes(v.shape[0])
    one = jnp.ones_like(v)
    return jnp.where(low, v, one) if hh == 0 else jnp.where(low, one, v)


def _normalise(o):
    return o / pltpu.roll(o, HEAD_DIM, 1)


def _softmax_pv(s, v, hh):
    e = jnp.exp2(s - s.max(axis=-1, keepdims=True)).astype(BF16)
    return _normalise(_dot(e, _with_ones(v, hh)))


def _low_lanes(n):
    return lax.broadcasted_iota(jnp.int32, (n, V7X_LANES), 1) < HEAD_DIM


def _mask_head(q, hh):
    low = _low_lanes(q.shape[0])
    zero = jnp.zeros_like(q)
    return jnp.where(low, q, zero) if hh == 0 else jnp.where(low, zero, q)


def _two_stage_steps(j, init, step):
    @pl.when(j == 0)
    def _():
        init(1)

    @pl.when(j % 2 == 0)
    def _():
        step(0, 1)

    @pl.when(j % 2 == 1)
    def _():
        step(1, 0)


def _na_first_key(i):
    first_key_row = jnp.clip(NA_Q_ROWS * i - NA_WIN_H // 2, 0, GRID_H - NA_K_ROWS)
    return pl.multiple_of(first_key_row * GRID_W, GRID_W)


def _na_kernel(q_ref, k_ref, kc_ref, bias_ref, v_ref, vc_ref, o_ref, e_ref, ec_ref):
    j = pl.program_id(0)
    k0_a = _na_first_key(jnp.minimum(j, NA_TILES - 1) % NA_BLOCKS)
    k0_b = _na_first_key(jnp.maximum(j - 1, 0) % NA_BLOCKS)

    def init(slot):
        e_ref[slot] = jnp.ones(e_ref.shape[1:], BF16)
        ec_ref[slot] = jnp.ones(ec_ref.shape[1:], BF16)

    def store_exp(slot, h, s_loc, s_ctx, m):
        e_ref[slot, h] = jnp.exp2(s_loc - m).astype(BF16)
        ec_ref[slot, h] = jnp.exp2(s_ctx - m).astype(BF16)

    def step(slot_a, slot_b):
        low = _low_lanes(NA_Q_TILE)
        pending = None
        o_pair = []
        for h in range(NA_HEADS):
            c, hh = divmod(h, 2)
            sl = slice(c * V7X_LANES, (c + 1) * V7X_LANES)
            qm = _mask_head(q_ref[0, :, sl], hh)
            s_loc = _dot_nt(qm, k_ref[0, pl.ds(k0_a, NA_K_TILE), sl]) + bias_ref[0, 0, h]
            s_ctx = _dot_nt(qm, kc_ref[0, :, sl])
            m = jnp.maximum(s_loc.max(axis=-1, keepdims=True), s_ctx.max(axis=-1, keepdims=True))
            if pending is not None:
                store_exp(slot_a, *pending)
            pending = (h, s_loc, s_ctx, m)
            vw = _with_ones(v_ref[0, pl.ds(k0_b, NA_K_TILE), sl], hh)
            vc = _with_ones(vc_ref[0, :, sl], hh)
            o_pair.append(_normalise(_dot(e_ref[slot_b, h], vw) + _dot(ec_ref[slot_b, h], vc)))
            if hh == 1:
                o_ref[0, :, sl] = jnp.where(low, o_pair[0], o_pair[1])
                o_pair = []
        store_exp(slot_a, *pending)

    _two_stage_steps(j, init, step)


def _na_block_picks():
    picks = np.full((3, NA_Q_ROWS, NA_K_ROWS), RPB_H, np.int32)
    for v, i in enumerate((0, 1, NA_BLOCKS - 1)):
        first_key_row = min(max(NA_Q_ROWS * i - NA_WIN_H // 2, 0), GRID_H - NA_K_ROWS)
        for a in range(NA_Q_ROWS):
            qr = NA_Q_ROWS * i + a
            rs = min(max(qr - NA_WIN_H // 2, 0), GRID_H - NA_WIN_H)
            for n in range(NA_K_ROWS):
                kr = first_key_row + n
                if rs <= kr < rs + NA_WIN_H:
                    picks[v, a, n] = kr - qr + NA_WIN_H - 1
    return picks


def _na_bias_kernel(ring_ref, o_ref, *, picks):
    ring = ring_ref[0, 0]
    qc = lax.broadcasted_iota(jnp.int32, (GRID_W, V7X_LANES), 0)
    kc = lax.broadcasted_iota(jnp.int32, (GRID_W, V7X_LANES), 1)
    cs = jnp.clip(qc - NA_WIN_W // 2, 0, GRID_W - NA_WIN_W)
    in_win = (kc >= cs) & (kc < cs + NA_WIN_W)
    blocks = []
    for r in range(RPB_H):
        rows = jnp.broadcast_to(ring[r:r + 1, :], (GRID_W, V7X_LANES))
        toe = pltpu.roll(rows, 0, 1, stride=1, stride_axis=0)
        blocks.append(jnp.where(in_win, toe * LOG2_E, NEG_INF)[:, :GRID_W])
    blocks.append(jnp.full((GRID_W, GRID_W), NEG_INF, F32))
    for v in range(picks.shape[0]):
        for a in range(NA_Q_ROWS):
            o_ref[0, v, 0, a * GRID_W:(a + 1) * GRID_W, :] = jnp.concatenate(
                [blocks[p] for p in picks[v, a]], axis=1)


def _na_bias_table(rpb):
    layers = rpb.shape[0]
    pad_rows = 16 - RPB_H
    ring = jnp.concatenate(
        [rpb[..., NA_WIN_W - 1:], jnp.zeros(rpb.shape[:3] + (V7X_LANES - RPB_W,), F32),
         rpb[..., :NA_WIN_W - 1]], axis=-1)
    ring = jnp.pad(ring, ((0, 0), (0, 0), (0, pad_rows), (0, 0)))
    return pl.pallas_call(
        functools.partial(_na_bias_kernel, picks=_na_block_picks()),
        grid=(layers, NA_HEADS),
        in_specs=[pl.BlockSpec((1, 1, RPB_H + pad_rows, V7X_LANES), lambda l, h: (l, h, 0, 0))],
        out_specs=pl.BlockSpec((1, 3, 1, NA_Q_TILE, NA_K_TILE), lambda l, h: (l, 0, h, 0, 0)),
        out_shape=jax.ShapeDtypeStruct((layers, 3, NA_HEADS, NA_Q_TILE, NA_K_TILE), F32),
        compiler_params=pltpu.CompilerParams(dimension_semantics=("parallel", "parallel")),
        name="na_bias_table",
    )(ring)


def _na_attention(qa, ka, va, bias, layer):
    tile_a = lambda j: jnp.minimum(j, NA_TILES - 1)
    tile_b = lambda j: jnp.maximum(j - 1, 0)

    def variant(j):
        i = tile_a(j) % NA_BLOCKS
        return (layer, jnp.where(i == 0, 0, jnp.where(i == NA_BLOCKS - 1, 2, 1)), 0, 0, 0)

    blk = (NA_Q_TILE * NA_WIDTH * (2 + 4) + 2 * (SEQ + CTX_LEN) * NA_WIDTH * 2
           + NA_HEADS * NA_Q_TILE * NA_K_TILE * 4)
    scratch = 2 * NA_HEADS * NA_Q_TILE * (NA_K_TILE + CTX_LEN) * 2
    scores = NA_HEADS * NA_Q_TILE * (NA_K_TILE + CTX_LEN) * 4
    return pl.pallas_call(
        _na_kernel,
        grid=(NA_TILES + 1,),
        in_specs=[
            pl.BlockSpec((1, NA_Q_TILE, NA_WIDTH), lambda j: (tile_a(j) // NA_BLOCKS, tile_a(j) % NA_BLOCKS, 0)),
            pl.BlockSpec((1, SEQ, NA_WIDTH), lambda j: (tile_a(j) // NA_BLOCKS, 0, 0)),
            pl.BlockSpec((1, CTX_LEN, NA_WIDTH), lambda j: (CTX_GROUP, tile_a(j) // NA_BLOCKS, 0)),
            pl.BlockSpec((1, 1, NA_HEADS, NA_Q_TILE, NA_K_TILE), variant),
            pl.BlockSpec((1, SEQ, NA_WIDTH), lambda j: (tile_b(j) // NA_BLOCKS, 0, 0)),
            pl.BlockSpec((1, CTX_LEN, NA_WIDTH), lambda j: (CTX_GROUP, tile_b(j) // NA_BLOCKS, 0)),
        ],
        out_specs=pl.BlockSpec((1, NA_Q_TILE, NA_WIDTH),
                               lambda j: (tile_b(j) // NA_BLOCKS, tile_b(j) % NA_BLOCKS, 0)),
        out_shape=jax.ShapeDtypeStruct((BATCH, SEQ, NA_WIDTH), F32),
        scratch_shapes=[pltpu.VMEM((2, NA_HEADS, NA_Q_TILE, NA_K_TILE), BF16),
                        pltpu.VMEM((2, NA_HEADS, NA_Q_TILE, CTX_LEN), BF16)],
        compiler_params=pltpu.CompilerParams(
            dimension_semantics=("arbitrary",),
            vmem_limit_bytes=_vmem_limit(blk, scratch, scores)),
        name="neighbourhood_attention",
    )(qa, ka, ka, bias, va, va)


def _gqa_stack(q, kvh):
    return jnp.concatenate(
        [_mask_head(q[:, c * V7X_LANES:(c + 1) * V7X_LANES], kvh) for c in range(GQA_GROUP)], axis=0)


def _gqa_unstack(o_kv0, o_kv1, n_q):
    low = _low_lanes(n_q)
    return [jnp.where(low, o_kv0[c * n_q:(c + 1) * n_q], o_kv1[c * n_q:(c + 1) * n_q])
            for c in range(GQA_GROUP)]


def _gqa_kernel(q_ref, kt_ref, kct_ref, v_ref, vc_ref, o_ref, e_ref):
    j = pl.program_id(0)
    chains = [(t, kvh) for t in range(GQA_Q_TILE // GQA_SUB_TILE) for kvh in range(GQA_KV_HEADS)]
    n_lat = SEQ // GQA_KEY_CHUNK
    n_chunks = n_lat + CTX_LEN // GQA_KEY_CHUNK

    def init(slot):
        e_ref[slot] = jnp.ones(e_ref.shape[1:], BF16)

    def step(slot_a, slot_b):
        v_ones = [(_with_ones(v_ref[0], kvh), _with_ones(vc_ref[0], kvh)) for kvh in range(GQA_KV_HEADS)]

        def chunk(ref_lat, ref_ctx, c, axis):
            src, cc = (ref_lat, c) if c < n_lat else (ref_ctx, c - n_lat)
            keys = slice(cc * GQA_KEY_CHUNK, (cc + 1) * GQA_KEY_CHUNK)
            return src[keys] if axis == 0 else src[0, :, keys]

        n_slots = len(chains) + 1
        n_units = len(chains) * n_chunks
        unit_at = {u: u for u in range(n_units)}
        scores, row_max, acc, outs = {}, {}, {}, {}
        for i in range(n_slots):
            live = i < len(chains)
            if live:
                t, kvh = chains[i]
                stack = _gqa_stack(q_ref[0, t * GQA_SUB_TILE:(t + 1) * GQA_SUB_TILE, :], kvh)
                scores[i], lane_max = [], None
            for c in range(n_chunks):
                keys = slice(c * GQA_KEY_CHUNK, (c + 1) * GQA_KEY_CHUNK)
                if live:
                    s = _dot(stack, chunk(kt_ref, kct_ref, c, 1))
                    scores[i].append(s)
                    for w in range(GQA_KEY_CHUNK // V7X_LANES):
                        part = s[:, w * V7X_LANES:(w + 1) * V7X_LANES]
                        lane_max = part if lane_max is None else jnp.maximum(lane_max, part)
                if i >= 1:
                    e_ref[slot_a, i - 1, :, keys] = jnp.exp2(scores[i - 1][c] - row_max[i - 1]).astype(BF16)
                u = unit_at.get(i * n_chunks + c)
                if u is not None:
                    b, cb = divmod(u, n_chunks)
                    tb, kvb = chains[b]
                    bkeys = slice(cb * GQA_KEY_CHUNK, (cb + 1) * GQA_KEY_CHUNK)
                    part = _dot(e_ref[slot_b, b, :, bkeys], chunk(v_ones[kvb][0], v_ones[kvb][1], cb, 0))
                    acc[b] = part if cb == 0 else acc[b] + part
                    if cb == n_chunks - 1:
                        outs[kvb] = _normalise(acc.pop(b))
                        if kvb == GQA_KV_HEADS - 1:
                            rows = slice(tb * GQA_SUB_TILE, (tb + 1) * GQA_SUB_TILE)
                            for col, oc in enumerate(_gqa_unstack(outs[0], outs[1], GQA_SUB_TILE)):
                                o_ref[0, rows, col * V7X_LANES:(col + 1) * V7X_LANES] = oc
            if i >= 1:
                del scores[i - 1]
            if live:
                row_max[i] = lane_max.max(axis=-1, keepdims=True)

    _two_stage_steps(j, init, step)


def _gqa_attention(qb, kbt, vb):
    tile_a = lambda j: jnp.minimum(j, GQA_TILES - 1)
    tile_b = lambda j: jnp.maximum(j - 1, 0)
    per_b = SEQ // GQA_Q_TILE
    chains = GQA_KV_HEADS * GQA_Q_TILE // GQA_SUB_TILE
    chain_rows = GQA_GROUP * GQA_SUB_TILE
    blk = (GQA_Q_TILE * GQA_WIDTH * (2 + 4) + 2 * (SEQ + CTX_LEN) * V7X_LANES * 2)
    scratch = 2 * chains * chain_rows * (SEQ + CTX_LEN) * 2
    scores = chain_rows * (SEQ + CTX_LEN) * 4
    return pl.pallas_call(
        _gqa_kernel,
        grid=(GQA_TILES + 1,),
        in_specs=[
            pl.BlockSpec((1, GQA_Q_TILE, GQA_WIDTH), lambda j: (tile_a(j) // per_b, tile_a(j) % per_b, 0)),
            pl.BlockSpec((1, GQA_KV_WIDTH, SEQ), lambda j: (tile_a(j) // per_b, 0, 0)),
            pl.BlockSpec((1, GQA_KV_WIDTH, CTX_LEN), lambda j: (CTX_GROUP, 0, tile_a(j) // per_b)),
            pl.BlockSpec((1, SEQ, GQA_KV_WIDTH), lambda j: (tile_b(j) // per_b, 0, 0)),
            pl.BlockSpec((1, CTX_LEN, GQA_KV_WIDTH), lambda j: (CTX_GROUP, tile_b(j) // per_b, 0)),
        ],
        out_specs=pl.BlockSpec((1, GQA_Q_TILE, GQA_WIDTH), lambda j: (tile_b(j) // per_b, tile_b(j) % per_b, 0)),
        out_shape=jax.ShapeDtypeStruct((BATCH, SEQ, GQA_WIDTH), F32),
        scratch_shapes=[pltpu.VMEM((2, chains, chain_rows, SEQ + CTX_LEN), BF16)],
        compiler_params=pltpu.CompilerParams(
            dimension_semantics=("arbitrary",),
            vmem_limit_bytes=_vmem_limit(blk, scratch, 4 * scores)),
        name="gqa_attention",
    )(qb, kbt, kbt, vb, vb)


def _ctx_kernel(qa_ref, ka_ref, va_ref, qb_ref, kbt_ref, vb_ref, oa_ref, ob_ref):
    low = _low_lanes(CTX_LEN)
    for c in range(NA_WIDTH // V7X_LANES):
        sl = slice(c * V7X_LANES, (c + 1) * V7X_LANES)
        q, k, v = qa_ref[0, :, sl], ka_ref[0, :, sl], va_ref[0, :, sl]
        o = [_softmax_pv(_dot_nt(_mask_head(q, hh), k), v, hh) for hh in range(2)]
        oa_ref[0, :, sl] = jnp.where(low, o[0], o[1])
    q = qb_ref[0]
    o = [_softmax_pv(_dot(_gqa_stack(q, kvh), kbt_ref[0]), vb_ref[0], kvh) for kvh in range(GQA_KV_HEADS)]
    for c, oc in enumerate(_gqa_unstack(o[0], o[1], CTX_LEN)):
        ob_ref[0, :, c * V7X_LANES:(c + 1) * V7X_LANES] = oc


def _ctx_attention(qa, ka, va, qb, kbt, vb):
    row = lambda b: (CTX_GROUP, b, 0)
    wide = pl.BlockSpec((1, CTX_LEN, NA_WIDTH), row)
    narrow = pl.BlockSpec((1, CTX_LEN, GQA_KV_WIDTH), row)
    narrow_t = pl.BlockSpec((1, GQA_KV_WIDTH, CTX_LEN), lambda b: (CTX_GROUP, 0, b))
    out = pl.BlockSpec((1, CTX_LEN, NA_WIDTH), lambda b: (0, b, 0))
    shape = jax.ShapeDtypeStruct((1, SEQ, NA_WIDTH), F32)
    blk = CTX_LEN * (4 * NA_WIDTH * 2 + 2 * GQA_KV_WIDTH * 2 + 2 * NA_WIDTH * 4)
    return pl.pallas_call(
        _ctx_kernel,
        grid=(BATCH,),
        in_specs=[wide, wide, wide, wide, narrow_t, narrow],
        out_specs=[out, out],
        out_shape=[shape, shape],
        compiler_params=pltpu.CompilerParams(
            dimension_semantics=("parallel",),
            vmem_limit_bytes=_vmem_limit(blk, 0, 16 * GQA_GROUP * CTX_LEN * CTX_LEN * 4)),
        name="context_attention",
    )(qa, ka, va, qb, kbt, vb)


def _outproj_kernel(*refs, split_ctx):
    if split_ctx:
        x_ref, oa_ref, ob_ref, oac_ref, obc_ref, mod_ref, ga_ref, gb_ref, w_ref, o_ref = refs
        is_ctx = pl.program_id(0) == CTX_GROUP
        oa = jnp.where(is_ctx, oac_ref[0], oa_ref[0])
        ob = jnp.where(is_ctx, obc_ref[0], ob_ref[0])
    else:
        x_ref, oa_ref, ob_ref, mod_ref, ga_ref, gb_ref, w_ref, o_ref = refs
        oa, ob = oa_ref[0], ob_ref[0]
    gate = mod_ref[0][:, 2 * D_MODEL:]
    ya = (_rms(oa) * ga_ref[...]).astype(BF16)
    yb = (_rms(ob) * gb_ref[...]).astype(BF16)
    y = _dot(ya, w_ref[:NA_WIDTH, :]) + _dot(yb, w_ref[NA_WIDTH:, :])
    o_ref[0] = x_ref[0] + gate * y


def _out_projection(x, oa, ob, ctx_o, mod, ga, gb, w_out_b):
    split_ctx = ctx_o is not None
    n_groups = N_GROUPS if split_ctx else BATCH
    row = lambda g, i: (g, i, 0)
    const2 = lambda g, i: (0, 0)
    last_tile = SEQ // ROW_TILE - 1
    lat = lambda g, i: (jnp.minimum(g, BATCH - 1), jnp.where(g == CTX_GROUP, last_tile, i), 0)
    cxt = lambda g, i: (0, jnp.where(g == CTX_GROUP, i, 0), 0)
    att = lambda imap: pl.BlockSpec((1, ROW_TILE, NA_WIDTH), imap)
    att_specs = [att(lat), att(lat), att(cxt), att(cxt)] if split_ctx else [att(row), att(row)]
    att_args = (oa, ob) + (tuple(ctx_o) if split_ctx else ())
    blk = ROW_TILE * (2 * D_MODEL + (2 + 2 * split_ctx) * NA_WIDTH) * 4
    return pl.pallas_call(
        functools.partial(_outproj_kernel, split_ctx=split_ctx),
        grid=(n_groups, SEQ // ROW_TILE),
        in_specs=[pl.BlockSpec((1, ROW_TILE, D_MODEL), row)] + att_specs + [
            pl.BlockSpec((1, 1, 3 * D_MODEL), lambda g, i: (g, 0, 1)),
            pl.BlockSpec((1, NA_WIDTH), const2),
            pl.BlockSpec((1, GQA_WIDTH), const2),
            pl.BlockSpec((NA_WIDTH + GQA_WIDTH, D_MODEL), const2, pipeline_mode=pl.Buffered(1)),
        ],
        out_specs=pl.BlockSpec((1, ROW_TILE, D_MODEL), row),
        out_shape=jax.ShapeDtypeStruct((n_groups, SEQ, D_MODEL), F32),
        compiler_params=pltpu.CompilerParams(
            dimension_semantics=("arbitrary", "arbitrary"),
            vmem_limit_bytes=_vmem_limit(blk, (NA_WIDTH + GQA_WIDTH) * D_MODEL * 2, blk)),
        name="out_projection_ctx" if split_ctx else "out_projection",
    )(x, *att_args, mod, ga.reshape(1, NA_WIDTH), gb.reshape(1, GQA_WIDTH), w_out_b)


def _gqa_head_permutation():
    p = np.arange(GQA_WIDTH)
    col, half, d = p // V7X_LANES, (p % V7X_LANES) // HEAD_DIM, p % HEAD_DIM
    return HEAD_DIM * (col + GQA_GROUP * half) + d


def _rope_tables():
    t = jnp.arange(SEQ, dtype=jnp.int32)
    row = (t // GRID_W).astype(F32)
    col = (t % GRID_W).astype(F32)
    axis_dim = HEAD_DIM // 2
    inv_freq = ROPE_THETA ** (-jnp.arange(0, axis_dim, 2, dtype=F32) / axis_dim)
    ang_r = row[:, None] * inv_freq[None, :]
    ang_c = col[:, None] * inv_freq[None, :]
    cos_r, sin_r, cos_c, sin_c = jnp.cos(ang_r), jnp.sin(ang_r), jnp.cos(ang_c), jnp.sin(ang_c)
    cos_h = jnp.concatenate([cos_r, cos_r, cos_c, cos_c], axis=-1)
    sin_h = jnp.concatenate([-sin_r, sin_r, -sin_c, sin_c], axis=-1)
    heads = V7X_LANES // HEAD_DIM
    cos2 = jnp.tile(cos_h, (1, heads))
    sin2 = jnp.tile(sin_h, (1, heads))
    return (jnp.stack([cos2, jnp.ones_like(cos2)]), jnp.stack([sin2, jnp.zeros_like(sin2)]))


def kernel(x, c, ctx, c_ctx, w_mod, b_mod, norm_ffn1, ffn1_w13, ffn1_w2, norm_mix, w_in, na_rpb,
           gqa_q_norm, gqa_k_norm, out_norm_a, out_norm_b, w_out, norm_ffn2, ffn2_w13, ffn2_w2, norm_f):
    perm = _gqa_head_permutation()
    cos_t, sin_t = _rope_tables()
    head_of_lane = np.arange(V7X_LANES) // HEAD_DIM
    ones_blk = np.asarray(head_of_lane[:, None] == head_of_lane[None, :], np.float32)
    ones2 = jnp.asarray(np.concatenate([ones_blk, ones_blk], axis=0), BF16)
    heads_per_col = V7X_LANES // HEAD_DIM
    bias_all = _na_bias_table(na_rpb)

    cond = jnp.zeros((COND_ROWS, D_MODEL), F32).at[:BATCH].set(c).at[CTX_GROUP].set(c_ctx)
    mod_all = _modulation(cond, w_mod, b_mod)

    w13b_1, w2b_1 = _cast_bf16(ffn1_w13), _cast_bf16(ffn1_w2)
    w13b_2, w2b_2 = _cast_bf16(ffn2_w13), _cast_bf16(ffn2_w2)

    xs = x
    for l in range(DEPTH):
        last = l == DEPTH - 1
        mod = mod_all[l].reshape(COND_ROWS, 1, N_MOD * D_MODEL)
        qb0 = 3 * NA_WIDTH
        w_in_l = w_in[l]
        w_in_b = jnp.concatenate(
            [w_in_l[:, :qb0], w_in_l[:, qb0:qb0 + GQA_WIDTH][:, perm], w_in_l[:, qb0 + GQA_WIDTH:]],
            axis=1).astype(BF16)
        w_out_l = w_out[l]
        w_out_b = jnp.concatenate([w_out_l[:NA_WIDTH], w_out_l[NA_WIDTH:][perm]], axis=0).astype(BF16)
        gq = jnp.tile(gqa_q_norm[l], heads_per_col).reshape(1, V7X_LANES)
        gk = jnp.tile(gqa_k_norm[l], heads_per_col).reshape(1, V7X_LANES)

        xs = _ffn_half(xs, mod, 0, norm_ffn1[l], w13b_1, w2b_1, l, N_GROUPS,
                       ctx=ctx.reshape(1, SEQ, D_MODEL) if l == 0 else None)
        qa, ka, va, qb, kb, vb = _in_projection(xs, mod, norm_mix[l], w_in_b, ones2, gq, gk,
                                                cos_t, sin_t)
        kbt = jnp.swapaxes(kb, 1, 2)
        oa = _na_attention(qa, ka, va, bias_all, l)
        ob = _gqa_attention(qb, kbt, vb)
        ctx_o = None if last else _ctx_attention(qa, ka, va, qb, kbt, vb)
        xs = _out_projection(xs, oa, ob, ctx_o, mod, out_norm_a[l], out_norm_b[l][perm], w_out_b)
        xs = _ffn_half(xs, mod, 2, norm_ffn2[l], w13b_2, w2b_2, l, BATCH if last else N_GROUPS,
                       final_g=norm_f if last else None)
    return xs
```

```python
import functools

import numpy as np
import jax
import jax.numpy as jnp
from jax import lax
from jax.experimental import pallas as pl
from jax.experimental.pallas import tpu as pltpu

F32 = jnp.float32
BF16 = jnp.bfloat16

D_MODEL = 1024
BATCH = 8
SEQ = 2048
DEPTH = 2
GRID_W = 64
GRID_H = SEQ // GRID_W
CTX_LEN = 256
HEAD_DIM = 64
NA_HEADS = 8
GQA_HEADS = 8
GQA_KV_HEADS = 2
GQA_GROUP = GQA_HEADS // GQA_KV_HEADS
NA_WIDTH = NA_HEADS * HEAD_DIM
GQA_WIDTH = GQA_HEADS * HEAD_DIM
GQA_KV_WIDTH = GQA_KV_HEADS * HEAD_DIM
IN_WIDTH = 3 * NA_WIDTH + GQA_WIDTH + 2 * GQA_KV_WIDTH
NA_WIN_H = 8
NA_WIN_W = 16
RPB_H = 2 * NA_WIN_H - 1
RPB_W = 2 * NA_WIN_W - 1
D_FF = 2816
N_MOD = 9
ROPE_THETA = 10000.0
EPS = 1e-6
NEG_INF = -1e30
LOG2_E = 1.4426950408889634
Q_SCALE = LOG2_E * HEAD_DIM ** -0.5

N_GROUPS = BATCH + 1
CTX_GROUP = BATCH
COND_ROWS = 16

V7X_LANES = 128
V7X_VMEM_BYTES = 64 * 1024 * 1024
V7X_VMEM_REQUEST_CAP = 56 * 1024 * 1024

ROW_TILE = 1024
SPLIT_ROW_TILE = 512
FF_CHUNK = 256
N_FF_CHUNKS = D_FF // FF_CHUNK
MOD_TILE = 1152
NA_Q_ROWS = 4
NA_Q_TILE = NA_Q_ROWS * GRID_W
NA_K_ROWS = NA_Q_ROWS + NA_WIN_H - 1
NA_K_TILE = NA_K_ROWS * GRID_W
NA_BLOCKS = GRID_H // NA_Q_ROWS
NA_TILES = BATCH * NA_BLOCKS
GQA_Q_TILE = 512
GQA_SUB_TILE = 64
GQA_TILES = BATCH * (SEQ // GQA_Q_TILE)
GQA_KEY_CHUNK = 256

assert BATCH * CTX_LEN == SEQ
assert D_FF % FF_CHUNK == 0 and SEQ % ROW_TILE == 0 and (N_MOD * D_MODEL) % MOD_TILE == 0
assert SEQ % SPLIT_ROW_TILE == 0
assert GRID_H % NA_Q_ROWS == 0 and SEQ % GQA_Q_TILE == 0 and GQA_Q_TILE % GQA_SUB_TILE == 0
assert SEQ % GQA_KEY_CHUNK == 0 and CTX_LEN % GQA_KEY_CHUNK == 0


def _vmem_limit(block_bytes, scratch_bytes, temp_bytes):
    need = 2 * block_bytes + scratch_bytes + temp_bytes
    return int(min(max(need, 16 * 1024 * 1024), V7X_VMEM_REQUEST_CAP))


def _rms(x):
    return x * lax.rsqrt(jnp.mean(x * x, axis=-1, keepdims=True) + EPS)


def _dot(a, b):
    return jnp.dot(a, b, preferred_element_type=F32)


def _dot_nt(a, b):
    return lax.dot_general(a, b, (((1,), (1,)), ((), ())), preferred_element_type=F32)


def _mod_kernel(c_ref, w_ref, b_ref, o_ref):
    c = c_ref[...]
    s = (c * jax.nn.sigmoid(c)).astype(BF16)
    o_ref[0] = _dot(s, w_ref[0].astype(BF16)) + b_ref[0]


def _modulation(cond, w_mod, b_mod):
    n = N_MOD * D_MODEL
    return pl.pallas_call(
        _mod_kernel,
        grid=(DEPTH, n // MOD_TILE),
        in_specs=[
            pl.BlockSpec((COND_ROWS, D_MODEL), lambda l, j: (0, 0)),
            pl.BlockSpec((1, D_MODEL, MOD_TILE), lambda l, j: (l, 0, j)),
            pl.BlockSpec((1, 1, MOD_TILE), lambda l, j: (l, 0, j)),
        ],
        out_specs=pl.BlockSpec((1, COND_ROWS, MOD_TILE), lambda l, j: (l, 0, j)),
        out_shape=jax.ShapeDtypeStruct((DEPTH, COND_ROWS, n), F32),
        compiler_params=pltpu.CompilerParams(
            dimension_semantics=("parallel", "parallel"),
            vmem_limit_bytes=_vmem_limit(D_MODEL * MOD_TILE * 4, 0, D_MODEL * MOD_TILE * 2)),
        name="adaln_modulation",
    )(cond, w_mod, b_mod.reshape(DEPTH, 1, n))


CAST_BLOCK_BYTES = 6 * 1024 * 1024


def _cast_kernel(w_ref, o_ref):
    o_ref[...] = w_ref[...].astype(BF16)


def _cast_bf16(w):
    layers, rows, cols = w.shape
    tile = rows
    while tile * cols * 4 > CAST_BLOCK_BYTES and tile % 32 == 0:
        tile //= 2
    spec = pl.BlockSpec((1, tile, cols), lambda l, i: (l, i, 0))
    return pl.pallas_call(
        _cast_kernel,
        grid=(layers, rows // tile),
        in_specs=[spec],
        out_specs=spec,
        out_shape=jax.ShapeDtypeStruct(w.shape, BF16),
        compiler_params=pltpu.CompilerParams(
            dimension_semantics=("parallel", "parallel"),
            vmem_limit_bytes=_vmem_limit(tile * cols * 6, 0, tile * cols * 2)),
        name="cast_bf16",
    )(w)


def _ffn_kernel(*refs, final, split_ctx):
    if split_ctx:
        x_ref, ctx_ref, mod_ref, g_ref, w13_ref, w2_ref, gf_ref, o_ref, u_ref = refs
        x = jnp.where(pl.program_id(0) == CTX_GROUP, ctx_ref[0], x_ref[0])
    else:
        x_ref, mod_ref, g_ref, w13_ref, w2_ref, gf_ref, o_ref, u_ref = refs
        x = x_ref[0]
    m = mod_ref[0]
    shift, scale, gate = m[:, :D_MODEL], m[:, D_MODEL:2 * D_MODEL], m[:, 2 * D_MODEL:]
    h = ((_rms(x) * g_ref[...]) * (1.0 + scale) + shift).astype(BF16)
    for j in range(N_FF_CHUNKS):
        a = _dot(h, w13_ref[0, :, j * FF_CHUNK:(j + 1) * FF_CHUNK])
        g = _dot(h, w13_ref[0, :, D_FF + j * FF_CHUNK:D_FF + (j + 1) * FF_CHUNK])
        u_ref[:, j * FF_CHUNK:(j + 1) * FF_CHUNK] = ((g * jax.nn.sigmoid(g)) * a).astype(BF16)
    y = _dot(u_ref[...], w2_ref[0])
    out = x + (0.5 * gate) * y
    if final:
        out = _rms(out) * gf_ref[...]
    o_ref[0] = out


def _ffn_half(x, mod, sub, norm_g, w13b, w2b, layer, n_groups, ctx=None, final_g=None):
    final = final_g is not None
    split_ctx = ctx is not None
    gf = final_g if final else norm_g
    tile = SPLIT_ROW_TILE if split_ctx else ROW_TILE
    blk = tile * D_MODEL * 4
    wbytes = 3 * D_FF * D_MODEL * 2
    last_tile = SEQ // tile - 1
    if split_ctx:
        x_specs = [
            pl.BlockSpec((1, tile, D_MODEL),
                         lambda g, i: (jnp.minimum(g, BATCH - 1), jnp.where(g == CTX_GROUP, last_tile, i), 0)),
            pl.BlockSpec((1, tile, D_MODEL), lambda g, i: (0, jnp.where(g == CTX_GROUP, i, 0), 0)),
        ]
        xs = (x, ctx)
    else:
        x_specs = [pl.BlockSpec((1, tile, D_MODEL), lambda g, i: (g, i, 0))]
        xs = (x,)
    return pl.pallas_call(
        functools.partial(_ffn_kernel, final=final, split_ctx=split_ctx),
        grid=(n_groups, SEQ // tile),
        in_specs=x_specs + [
            pl.BlockSpec((1, 1, 3 * D_MODEL), lambda g, i: (g, 0, sub)),
            pl.BlockSpec((1, D_MODEL), lambda g, i: (0, 0)),
            pl.BlockSpec((1, D_MODEL, 2 * D_FF), lambda g, i: (layer, 0, 0), pipeline_mode=pl.Buffered(1)),
            pl.BlockSpec((1, D_FF, D_MODEL), lambda g, i: (layer, 0, 0), pipeline_mode=pl.Buffered(1)),
            pl.BlockSpec((1, D_MODEL), lambda g, i: (0, 0)),
        ],
        out_specs=pl.BlockSpec((1, tile, D_MODEL), lambda g, i: (g, i, 0)),
        out_shape=jax.ShapeDtypeStruct((n_groups, SEQ, D_MODEL), F32),
        scratch_shapes=[pltpu.VMEM((tile, D_FF), BF16)],
        compiler_params=pltpu.CompilerParams(
            dimension_semantics=("arbitrary", "arbitrary"),
            vmem_limit_bytes=_vmem_limit((2 + split_ctx) * blk, wbytes + tile * D_FF * 2, 3 * blk)),
        name="ffn_half_final" if final else ("ffn_half_first" if split_ctx else "ffn_half"),
    )(*xs, mod, norm_g.reshape(1, D_MODEL), w13b, w2b, gf.reshape(1, D_MODEL))


def _norm_rope(x, ones2, gain, cos, sin, bit16):
    sq = x * x
    hi = sq.astype(BF16)
    lo = (sq - hi.astype(F32)).astype(BF16)
    ssum = _dot(jnp.concatenate([hi, lo], axis=1), ones2)
    y = (x * lax.rsqrt(ssum * (1.0 / HEAD_DIM) + EPS)) * gain
    quarter = HEAD_DIM // 4
    partner = jnp.where(bit16, pltpu.roll(y, quarter, 1), pltpu.roll(y, V7X_LANES - quarter, 1))
    return y * cos + partner * sin


def _proj_kernel(x_ref, mod_ref, g_ref, w_ref, ones_ref, gq_ref, gk_ref, cos_ref, sin_ref,
                 qa_ref, ka_ref, va_ref, qb_ref, kb_ref, vb_ref):
    x = x_ref[0]
    m = mod_ref[0]
    shift, scale = m[:, :D_MODEL], m[:, D_MODEL:2 * D_MODEL]
    h = ((_rms(x) * g_ref[...]) * (1.0 + scale) + shift).astype(BF16)
    qb0 = 3 * NA_WIDTH
    kb0 = qb0 + GQA_WIDTH
    cos = cos_ref[0]
    sin = sin_ref[0]
    ones2 = ones_ref[...]
    lane = lax.broadcasted_iota(jnp.int32, (ROW_TILE, V7X_LANES), 1)
    bit16 = (lane & (HEAD_DIM // 4)) != 0
    gq = gq_ref[...] * Q_SCALE
    gk = gk_ref[...]
    q = _dot(h, w_ref[:, qb0:kb0])
    for c in range(GQA_WIDTH // V7X_LANES):
        sl = slice(c * V7X_LANES, (c + 1) * V7X_LANES)
        qb_ref[0, :, sl] = _norm_rope(q[:, sl], ones2, gq, cos, sin, bit16).astype(BF16)
    kv = _dot(h, w_ref[:, kb0:kb0 + 2 * GQA_KV_WIDTH])
    kb_ref[0] = _norm_rope(kv[:, :GQA_KV_WIDTH], ones2, gk, cos, sin, bit16).astype(BF16)
    vb_ref[0] = kv[:, GQA_KV_WIDTH:].astype(BF16)
    qa_ref[0] = (_dot(h, w_ref[:, 0:NA_WIDTH]) * Q_SCALE).astype(BF16)
    ka_ref[0] = _dot(h, w_ref[:, NA_WIDTH:2 * NA_WIDTH]).astype(BF16)
    va_ref[0] = _dot(h, w_ref[:, 2 * NA_WIDTH:3 * NA_WIDTH]).astype(BF16)


def _in_projection(x, mod, norm_g, w_in_b, ones2, gq, gk, cos_t, sin_t):
    blk_in = ROW_TILE * D_MODEL * 4
    blk_out = ROW_TILE * IN_WIDTH * 2
    row = lambda g, i: (g, i, 0)
    tab = lambda g, i: (jnp.where(g == CTX_GROUP, 1, 0), i, 0)
    const2 = lambda g, i: (0, 0)
    shapes = [jax.ShapeDtypeStruct((N_GROUPS, SEQ, w), BF16)
              for w in (NA_WIDTH, NA_WIDTH, NA_WIDTH, GQA_WIDTH, GQA_KV_WIDTH, GQA_KV_WIDTH)]
    return pl.pallas_call(
        _proj_kernel,
        grid=(N_GROUPS, SEQ // ROW_TILE),
        in_specs=[
            pl.BlockSpec((1, ROW_TILE, D_MODEL), row),
            pl.BlockSpec((1, 1, 3 * D_MODEL), lambda g, i: (g, 0, 1)),
            pl.BlockSpec((1, D_MODEL), const2),
            pl.BlockSpec((D_MODEL, IN_WIDTH), const2, pipeline_mode=pl.Buffered(1)),
            pl.BlockSpec((2 * V7X_LANES, V7X_LANES), const2),
            pl.BlockSpec((1, V7X_LANES), const2),
            pl.BlockSpec((1, V7X_LANES), const2),
            pl.BlockSpec((1, ROW_TILE, V7X_LANES), tab),
            pl.BlockSpec((1, ROW_TILE, V7X_LANES), tab),
        ],
        out_specs=[pl.BlockSpec((1, ROW_TILE, s.shape[-1]), row) for s in shapes],
        out_shape=shapes,
        compiler_params=pltpu.CompilerParams(
            dimension_semantics=("parallel", "parallel"),
            vmem_limit_bytes=_vmem_limit(blk_in + blk_out, D_MODEL * IN_WIDTH * 2, 3 * blk_in)),
        name="in_projection",
    )(x, mod, norm_g.reshape(1, D_MODEL), w_in_b, ones2, gq, gk, cos_t, sin_t)


def _with_ones(v, hh):
    low = _low_lanes(v.shape[0])
    one = jnp.ones_like(v)
    return jnp.where(low, v, one) if hh == 0 else jnp.where(low, one, v)


def _normalise(o):
    return o / pltpu.roll(o, HEAD_DIM, 1)


def _softmax_pv(s, v, hh):
    e = jnp.exp2(s - s.max(axis=-1, keepdims=True)).astype(BF16)
    return _normalise(_dot(e, _with_ones(v, hh)))


def _low_lanes(n):
    return lax.broadcasted_iota(jnp.int32, (n, V7X_LANES), 1) < HEAD_DIM


def _mask_head(q, hh):
    low = _low_lanes(q.shape[0])
    zero = jnp.zeros_like(q)
    return jnp.where(low, q, zero) if hh == 0 else jnp.where(low, zero, q)


def _two_stage_steps(j, init, step):
    @pl.when(j == 0)
    def _():
        init(1)

    @pl.when(j % 2 == 0)
    def _():
        step(0, 1)

    @pl.when(j % 2 == 1)
    def _():
        step(1, 0)


def _na_first_key(i):
    first_key_row = jnp.clip(NA_Q_ROWS * i - NA_WIN_H // 2, 0, GRID_H - NA_K_ROWS)
    return pl.multiple_of(first_key_row * GRID_W, GRID_W)


def _na_kernel(q_ref, k_ref, kc_ref, bias_ref, v_ref, vc_ref, o_ref, e_ref, ec_ref):
    j = pl.program_id(0)
    k0_a = _na_first_key(jnp.minimum(j, NA_TILES - 1) % NA_BLOCKS)
    k0_b = _na_first_key(jnp.maximum(j - 1, 0) % NA_BLOCKS)

    def init(slot):
        e_ref[slot] = jnp.ones(e_ref.shape[1:], BF16)
        ec_ref[slot] = jnp.ones(ec_ref.shape[1:], BF16)

    def store_exp(slot, h, s_loc, s_ctx, m):
        e_ref[slot, h] = jnp.exp2(s_loc - m).astype(BF16)
        ec_ref[slot, h] = jnp.exp2(s_ctx - m).astype(BF16)

    def step(slot_a, slot_b):
        low = _low_lanes(NA_Q_TILE)
        pending = None
        o_pair = []
        for h in range(NA_HEADS):
            c, hh = divmod(h, 2)
            sl = slice(c * V7X_LANES, (c + 1) * V7X_LANES)
            qm = _mask_head(q_ref[0, :, sl], hh)
            s_loc = _dot_nt(qm, k_ref[0, pl.ds(k0_a, NA_K_TILE), sl]) + bias_ref[0, 0, h]
            s_ctx = _dot_nt(qm, kc_ref[0, :, sl])
            m = jnp.maximum(s_loc.max(axis=-1, keepdims=True), s_ctx.max(axis=-1, keepdims=True))
            if pending is not None:
                store_exp(slot_a, *pending)
            pending = (h, s_loc, s_ctx, m)
            vw = _with_ones(v_ref[0, pl.ds(k0_b, NA_K_TILE), sl], hh)
            vc = _with_ones(vc_ref[0, :, sl], hh)
            o_pair.append(_normalise(_dot(e_ref[slot_b, h], vw) + _dot(ec_ref[slot_b, h], vc)))
            if hh == 1:
                o_ref[0, :, sl] = jnp.where(low, o_pair[0], o_pair[1])
                o_pair = []
        store_exp(slot_a, *pending)

    _two_stage_steps(j, init, step)


def _na_block_picks():
    picks = np.full((3, NA_Q_ROWS, NA_K_ROWS), RPB_H, np.int32)
    for v, i in enumerate((0, 1, NA_BLOCKS - 1)):
        first_key_row = min(max(NA_Q_ROWS * i - NA_WIN_H // 2, 0), GRID_H - NA_K_ROWS)
        for a in range(NA_Q_ROWS):
            qr = NA_Q_ROWS * i + a
            rs = min(max(qr - NA_WIN_H // 2, 0), GRID_H - NA_WIN_H)
            for n in range(NA_K_ROWS):
                kr = first_key_row + n
                if rs <= kr < rs + NA_WIN_H:
                    picks[v, a, n] = kr - qr + NA_WIN_H - 1
    return picks


def _na_bias_kernel(ring_ref, o_ref, *, picks):
    ring = ring_ref[0, 0]
    qc = lax.broadcasted_iota(jnp.int32, (GRID_W, V7X_LANES), 0)
    kc = lax.broadcasted_iota(jnp.int32, (GRID_W, V7X_LANES), 1)
    cs = jnp.clip(qc - NA_WIN_W // 2, 0, GRID_W - NA_WIN_W)
    in_win = (kc >= cs) & (kc < cs + NA_WIN_W)
    blocks = []
    for r in range(RPB_H):
        rows = jnp.broadcast_to(ring[r:r + 1, :], (GRID_W, V7X_LANES))
        toe = pltpu.roll(rows, 0, 1, stride=1, stride_axis=0)
        blocks.append(jnp.where(in_win, toe * LOG2_E, NEG_INF)[:, :GRID_W])
    blocks.append(jnp.full((GRID_W, GRID_W), NEG_INF, F32))
    for v in range(picks.shape[0]):
        for a in range(NA_Q_ROWS):
            o_ref[0, v, 0, a * GRID_W:(a + 1) * GRID_W, :] = jnp.concatenate(
                [blocks[p] for p in picks[v, a]], axis=1)


def _na_bias_table(rpb):
    layers = rpb.shape[0]
    pad_rows = 16 - RPB_H
    ring = jnp.concatenate(
        [rpb[..., NA_WIN_W - 1:], jnp.zeros(rpb.shape[:3] + (V7X_LANES - RPB_W,), F32),
         rpb[..., :NA_WIN_W - 1]], axis=-1)
    ring = jnp.pad(ring, ((0, 0), (0, 0), (0, pad_rows), (0, 0)))
    return pl.pallas_call(
        functools.partial(_na_bias_kernel, picks=_na_block_picks()),
        grid=(layers, NA_HEADS),
        in_specs=[pl.BlockSpec((1, 1, RPB_H + pad_rows, V7X_LANES), lambda l, h: (l, h, 0, 0))],
        out_specs=pl.BlockSpec((1, 3, 1, NA_Q_TILE, NA_K_TILE), lambda l, h: (l, 0, h, 0, 0)),
        out_shape=jax.ShapeDtypeStruct((layers, 3, NA_HEADS, NA_Q_TILE, NA_K_TILE), F32),
        compiler_params=pltpu.CompilerParams(dimension_semantics=("parallel", "parallel")),
        name="na_bias_table",
    )(ring)


def _na_attention(qa, ka, va, bias, layer):
    tile_a = lambda j: jnp.minimum(j, NA_TILES - 1)
    tile_b = lambda j: jnp.maximum(j - 1, 0)

    def variant(j):
        i = tile_a(j) % NA_BLOCKS
        return (layer, jnp.where(i == 0, 0, jnp.where(i == NA_BLOCKS - 1, 2, 1)), 0, 0, 0)

    blk = (NA_Q_TILE * NA_WIDTH * (2 + 4) + 2 * (SEQ + CTX_LEN) * NA_WIDTH * 2
           + NA_HEADS * NA_Q_TILE * NA_K_TILE * 4)
    scratch = 2 * NA_HEADS * NA_Q_TILE * (NA_K_TILE + CTX_LEN) * 2
    scores = NA_HEADS * NA_Q_TILE * (NA_K_TILE + CTX_LEN) * 4
    return pl.pallas_call(
        _na_kernel,
        grid=(NA_TILES + 1,),
        in_specs=[
            pl.BlockSpec((1, NA_Q_TILE, NA_WIDTH), lambda j: (tile_a(j) // NA_BLOCKS, tile_a(j) % NA_BLOCKS, 0)),
            pl.BlockSpec((1, SEQ, NA_WIDTH), lambda j: (tile_a(j) // NA_BLOCKS, 0, 0)),
            pl.BlockSpec((1, CTX_LEN, NA_WIDTH), lambda j: (CTX_GROUP, tile_a(j) // NA_BLOCKS, 0)),
            pl.BlockSpec((1, 1, NA_HEADS, NA_Q_TILE, NA_K_TILE), variant),
            pl.BlockSpec((1, SEQ, NA_WIDTH), lambda j: (tile_b(j) // NA_BLOCKS, 0, 0)),
            pl.BlockSpec((1, CTX_LEN, NA_WIDTH), lambda j: (CTX_GROUP, tile_b(j) // NA_BLOCKS, 0)),
        ],
        out_specs=pl.BlockSpec((1, NA_Q_TILE, NA_WIDTH),
                               lambda j: (tile_b(j) // NA_BLOCKS, tile_b(j) % NA_BLOCKS, 0)),
        out_shape=jax.ShapeDtypeStruct((BATCH, SEQ, NA_WIDTH), F32),
        scratch_shapes=[pltpu.VMEM((2, NA_HEADS, NA_Q_TILE, NA_K_TILE), BF16),
                        pltpu.VMEM((2, NA_HEADS, NA_Q_TILE, CTX_LEN), BF16)],
        compiler_params=pltpu.CompilerParams(
            dimension_semantics=("arbitrary",),
            vmem_limit_bytes=_vmem_limit(blk, scratch, scores)),
        name="neighbourhood_attention",
    )(qa, ka, ka, bias, va, va)


def _gqa_stack(q, kvh):
    return jnp.concatenate(
        [_mask_head(q[:, c * V7X_LANES:(c + 1) * V7X_LANES], kvh) for c in range(GQA_GROUP)], axis=0)


def _gqa_unstack(o_kv0, o_kv1, n_q):
    low = _low_lanes(n_q)
    return [jnp.where(low, o_kv0[c * n_q:(c + 1) * n_q], o_kv1[c * n_q:(c + 1) * n_q])
            for c in range(GQA_GROUP)]


def _gqa_kernel(q_ref, kt_ref, kct_ref, v_ref, vc_ref, o_ref, e_ref):
    j = pl.program_id(0)
    chains = [(t, kvh) for t in range(GQA_Q_TILE // GQA_SUB_TILE) for kvh in range(GQA_KV_HEADS)]
    n_lat = SEQ // GQA_KEY_CHUNK
    n_chunks = n_lat + CTX_LEN // GQA_KEY_CHUNK

    def chunk(ref_lat, ref_ctx, c, axis):
        src, cc = (ref_lat, c) if c < n_lat else (ref_ctx, c - n_lat)
        keys = slice(cc * GQA_KEY_CHUNK, (cc + 1) * GQA_KEY_CHUNK)
        return src[keys] if axis == 0 else src[0, :, keys]

    def step(scores_of_tile_j, values_of_tile_before):
        if values_of_tile_before:
            v_ones = [(_with_ones(v_ref[0], kvh), _with_ones(vc_ref[0], kvh)) for kvh in range(GQA_KV_HEADS)]
        scores, row_max, outs = {}, {}, {}
        for i in range(len(chains) + scores_of_tile_j):
            live = i < len(chains)
            if live:
                t, kvh = chains[i]
                rows = slice(t * GQA_SUB_TILE, (t + 1) * GQA_SUB_TILE)
            if live and scores_of_tile_j:
                stack = _gqa_stack(q_ref[0, rows, :], kvh)
                scores[i], lane_max = [], None
            acc = None
            for c in range(n_chunks):
                keys = slice(c * GQA_KEY_CHUNK, (c + 1) * GQA_KEY_CHUNK)
                if live and scores_of_tile_j:
                    s = _dot(stack, chunk(kt_ref, kct_ref, c, 1))
                    scores[i].append(s)
                    for w in range(GQA_KEY_CHUNK // V7X_LANES):
                        part = s[:, w * V7X_LANES:(w + 1) * V7X_LANES]
                        lane_max = part if lane_max is None else jnp.maximum(lane_max, part)
                if i >= 1 and scores_of_tile_j:
                    e_ref[i - 1, :, keys] = jnp.exp2(scores[i - 1][c] - row_max[i - 1]).astype(BF16)
                if live and values_of_tile_before:
                    part = _dot(e_ref[i, :, keys], chunk(v_ones[kvh][0], v_ones[kvh][1], c, 0))
                    acc = part if acc is None else acc + part
            if i >= 1 and scores_of_tile_j:
                del scores[i - 1]
            if live and scores_of_tile_j:
                row_max[i] = lane_max.max(axis=-1, keepdims=True)
            if live and values_of_tile_before:
                outs[kvh] = _normalise(acc)
                if kvh == GQA_KV_HEADS - 1:
                    for col, oc in enumerate(_gqa_unstack(outs[0], outs[1], GQA_SUB_TILE)):
                        o_ref[0, rows, col * V7X_LANES:(col + 1) * V7X_LANES] = oc

    @pl.when(j == 0)
    def _():
        step(True, False)

    @pl.when((j > 0) & (j < GQA_TILES))
    def _():
        step(True, True)

    @pl.when(j == GQA_TILES)
    def _():
        step(False, True)


def _gqa_attention(qb, kbt, vb):
    tile_a = lambda j: jnp.minimum(j, GQA_TILES - 1)
    tile_b = lambda j: jnp.maximum(j - 1, 0)
    per_b = SEQ // GQA_Q_TILE
    chains = GQA_KV_HEADS * GQA_Q_TILE // GQA_SUB_TILE
    chain_rows = GQA_GROUP * GQA_SUB_TILE
    blk = (GQA_Q_TILE * GQA_WIDTH * (2 + 4) + 2 * (SEQ + CTX_LEN) * V7X_LANES * 2)
    scratch = chains * chain_rows * (SEQ + CTX_LEN) * 2
    scores = chain_rows * (SEQ + CTX_LEN) * 4
    return pl.pallas_call(
        _gqa_kernel,
        grid=(GQA_TILES + 1,),
        in_specs=[
            pl.BlockSpec((1, GQA_Q_TILE, GQA_WIDTH), lambda j: (tile_a(j) // per_b, tile_a(j) % per_b, 0)),
            pl.BlockSpec((1, GQA_KV_WIDTH, SEQ), lambda j: (tile_a(j) // per_b, 0, 0)),
            pl.BlockSpec((1, GQA_KV_WIDTH, CTX_LEN), lambda j: (CTX_GROUP, 0, tile_a(j) // per_b)),
            pl.BlockSpec((1, SEQ, GQA_KV_WIDTH), lambda j: (tile_b(j) // per_b, 0, 0)),
            pl.BlockSpec((1, CTX_LEN, GQA_KV_WIDTH), lambda j: (CTX_GROUP, tile_b(j) // per_b, 0)),
        ],
        out_specs=pl.BlockSpec((1, GQA_Q_TILE, GQA_WIDTH), lambda j: (tile_b(j) // per_b, tile_b(j) % per_b, 0)),
        out_shape=jax.ShapeDtypeStruct((BATCH, SEQ, GQA_WIDTH), F32),
        scratch_shapes=[pltpu.VMEM((chains, chain_rows, SEQ + CTX_LEN), BF16)],
        compiler_params=pltpu.CompilerParams(
            dimension_semantics=("arbitrary",),
            vmem_limit_bytes=_vmem_limit(blk, scratch, 4 * scores)),
        name="gqa_attention",
    )(qb, kbt, kbt, vb, vb)


def _ctx_kernel(qa_ref, ka_ref, va_ref, qb_ref, kbt_ref, vb_ref, oa_ref, ob_ref):
    low = _low_lanes(CTX_LEN)
    for c in range(NA_WIDTH // V7X_LANES):
        sl = slice(c * V7X_LANES, (c + 1) * V7X_LANES)
        q, k, v = qa_ref[0, :, sl], ka_ref[0, :, sl], va_ref[0, :, sl]
        o = [_softmax_pv(_dot_nt(_mask_head(q, hh), k), v, hh) for hh in range(2)]
        oa_ref[0, :, sl] = jnp.where(low, o[0], o[1])
    q = qb_ref[0]
    o = [_softmax_pv(_dot(_gqa_stack(q, kvh), kbt_ref[0]), vb_ref[0], kvh) for kvh in range(GQA_KV_HEADS)]
    for c, oc in enumerate(_gqa_unstack(o[0], o[1], CTX_LEN)):
        ob_ref[0, :, c * V7X_LANES:(c + 1) * V7X_LANES] = oc


def _ctx_attention(qa, ka, va, qb, kbt, vb):
    row = lambda b: (CTX_GROUP, b, 0)
    wide = pl.BlockSpec((1, CTX_LEN, NA_WIDTH), row)
    narrow = pl.BlockSpec((1, CTX_LEN, GQA_KV_WIDTH), row)
    narrow_t = pl.BlockSpec((1, GQA_KV_WIDTH, CTX_LEN), lambda b: (CTX_GROUP, 0, b))
    out = pl.BlockSpec((1, CTX_LEN, NA_WIDTH), lambda b: (0, b, 0))
    shape = jax.ShapeDtypeStruct((1, SEQ, NA_WIDTH), F32)
    blk = CTX_LEN * (4 * NA_WIDTH * 2 + 2 * GQA_KV_WIDTH * 2 + 2 * NA_WIDTH * 4)
    return pl.pallas_call(
        _ctx_kernel,
        grid=(BATCH,),
        in_specs=[wide, wide, wide, wide, narrow_t, narrow],
        out_specs=[out, out],
        out_shape=[shape, shape],
        compiler_params=pltpu.CompilerParams(
            dimension_semantics=("parallel",),
            vmem_limit_bytes=_vmem_limit(blk, 0, 16 * GQA_GROUP * CTX_LEN * CTX_LEN * 4)),
        name="context_attention",
    )(qa, ka, va, qb, kbt, vb)


def _outproj_kernel(*refs, split_ctx):
    if split_ctx:
        x_ref, oa_ref, ob_ref, oac_ref, obc_ref, mod_ref, ga_ref, gb_ref, w_ref, o_ref = refs
        is_ctx = pl.program_id(0) == CTX_GROUP
        oa = jnp.where(is_ctx, oac_ref[0], oa_ref[0])
        ob = jnp.where(is_ctx, obc_ref[0], ob_ref[0])
    else:
        x_ref, oa_ref, ob_ref, mod_ref, ga_ref, gb_ref, w_ref, o_ref = refs
        oa, ob = oa_ref[0], ob_ref[0]
    gate = mod_ref[0][:, 2 * D_MODEL:]
    ya = (_rms(oa) * ga_ref[...]).astype(BF16)
    yb = (_rms(ob) * gb_ref[...]).astype(BF16)
    y = _dot(ya, w_ref[:NA_WIDTH, :]) + _dot(yb, w_ref[NA_WIDTH:, :])
    o_ref[0] = x_ref[0] + gate * y


def _out_projection(x, oa, ob, ctx_o, mod, ga, gb, w_out_b):
    split_ctx = ctx_o is not None
    n_groups = N_GROUPS if split_ctx else BATCH
    row = lambda g, i: (g, i, 0)
    const2 = lambda g, i: (0, 0)
    last_tile = SEQ // ROW_TILE - 1
    lat = lambda g, i: (jnp.minimum(g, BATCH - 1), jnp.where(g == CTX_GROUP, last_tile, i), 0)
    cxt = lambda g, i: (0, jnp.where(g == CTX_GROUP, i, 0), 0)
    att = lambda imap: pl.BlockSpec((1, ROW_TILE, NA_WIDTH), imap)
    att_specs = [att(lat), att(lat), att(cxt), att(cxt)] if split_ctx else [att(row), att(row)]
    att_args = (oa, ob) + (tuple(ctx_o) if split_ctx else ())
    blk = ROW_TILE * (2 * D_MODEL + (2 + 2 * split_ctx) * NA_WIDTH) * 4
    return pl.pallas_call(
        functools.partial(_outproj_kernel, split_ctx=split_ctx),
        grid=(n_groups, SEQ // ROW_TILE),
        in_specs=[pl.BlockSpec((1, ROW_TILE, D_MODEL), row)] + att_specs + [
            pl.BlockSpec((1, 1, 3 * D_MODEL), lambda g, i: (g, 0, 1)),
            pl.BlockSpec((1, NA_WIDTH), const2),
            pl.BlockSpec((1, GQA_WIDTH), const2),
            pl.BlockSpec((NA_WIDTH + GQA_WIDTH, D_MODEL), const2, pipeline_mode=pl.Buffered(1)),
        ],
        out_specs=pl.BlockSpec((1, ROW_TILE, D_MODEL), row),
        out_shape=jax.ShapeDtypeStruct((n_groups, SEQ, D_MODEL), F32),
        compiler_params=pltpu.CompilerParams(
            dimension_semantics=("arbitrary", "arbitrary"),
            vmem_limit_bytes=_vmem_limit(blk, (NA_WIDTH + GQA_WIDTH) * D_MODEL * 2, blk)),
        name="out_projection_ctx" if split_ctx else "out_projection",
    )(x, *att_args, mod, ga.reshape(1, NA_WIDTH), gb.reshape(1, GQA_WIDTH), w_out_b)


def _gqa_head_permutation():
    p = np.arange(GQA_WIDTH)
    col, half, d = p // V7X_LANES, (p % V7X_LANES) // HEAD_DIM, p % HEAD_DIM
    return HEAD_DIM * (col + GQA_GROUP * half) + d


def _rope_tables():
    t = jnp.arange(SEQ, dtype=jnp.int32)
    row = (t // GRID_W).astype(F32)
    col = (t % GRID_W).astype(F32)
    axis_dim = HEAD_DIM // 2
    inv_freq = ROPE_THETA ** (-jnp.arange(0, axis_dim, 2, dtype=F32) / axis_dim)
    ang_r = row[:, None] * inv_freq[None, :]
    ang_c = col[:, None] * inv_freq[None, :]
    cos_r, sin_r, cos_c, sin_c = jnp.cos(ang_r), jnp.sin(ang_r), jnp.cos(ang_c), jnp.sin(ang_c)
    cos_h = jnp.concatenate([cos_r, cos_r, cos_c, cos_c], axis=-1)
    sin_h = jnp.concatenate([-sin_r, sin_r, -sin_c, sin_c], axis=-1)
    heads = V7X_LANES // HEAD_DIM
    cos2 = jnp.tile(cos_h, (1, heads))
    sin2 = jnp.tile(sin_h, (1, heads))
    return (jnp.stack([cos2, jnp.ones_like(cos2)]), jnp.stack([sin2, jnp.zeros_like(sin2)]))


def kernel(x, c, ctx, c_ctx, w_mod, b_mod, norm_ffn1, ffn1_w13, ffn1_w2, norm_mix, w_in, na_rpb,
           gqa_q_norm, gqa_k_norm, out_norm_a, out_norm_b, w_out, norm_ffn2, ffn2_w13, ffn2_w2, norm_f):
    perm = _gqa_head_permutation()
    cos_t, sin_t = _rope_tables()
    head_of_lane = np.arange(V7X_LANES) // HEAD_DIM
    ones_blk = np.asarray(head_of_lane[:, None] == head_of_lane[None, :], np.float32)
    ones2 = jnp.asarray(np.concatenate([ones_blk, ones_blk], axis=0), BF16)
    heads_per_col = V7X_LANES // HEAD_DIM
    bias_all = _na_bias_table(na_rpb)

    cond = jnp.zeros((COND_ROWS, D_MODEL), F32).at[:BATCH].set(c).at[CTX_GROUP].set(c_ctx)
    mod_all = _modulation(cond, w_mod, b_mod)

    w13b_1, w2b_1 = _cast_bf16(ffn1_w13), _cast_bf16(ffn1_w2)
    w13b_2, w2b_2 = _cast_bf16(ffn2_w13), _cast_bf16(ffn2_w2)

    xs = x
    for l in range(DEPTH):
        last = l == DEPTH - 1
        mod = mod_all[l].reshape(COND_ROWS, 1, N_MOD * D_MODEL)
        qb0 = 3 * NA_WIDTH
        w_in_l = w_in[l]
        w_in_b = jnp.concatenate(
            [w_in_l[:, :qb0], w_in_l[:, qb0:qb0 + GQA_WIDTH][:, perm], w_in_l[:, qb0 + GQA_WIDTH:]],
            axis=1).astype(BF16)
        w_out_l = w_out[l]
        w_out_b = jnp.concatenate([w_out_l[:NA_WIDTH], w_out_l[NA_WIDTH:][perm]], axis=0).astype(BF16)
        gq = jnp.tile(gqa_q_norm[l], heads_per_col).reshape(1, V7X_LANES)
        gk = jnp.tile(gqa_k_norm[l], heads_per_col).reshape(1, V7X_LANES)

        xs = _ffn_half(xs, mod, 0, norm_ffn1[l], w13b_1, w2b_1, l, N_GROUPS,
                       ctx=ctx.reshape(1, SEQ, D_MODEL) if l == 0 else None)
        qa, ka, va, qb, kb, vb = _in_projection(xs, mod, norm_mix[l], w_in_b, ones2, gq, gk,
                                                cos_t, sin_t)
        kbt = jnp.swapaxes(kb, 1, 2)
        oa = _na_attention(qa, ka, va, bias_all, l)
        ob = _gqa_attention(qb, kbt, vb)
        ctx_o = None if last else _ctx_attention(qa, ka, va, qb, kbt, vb)
        xs = _out_projection(xs, oa, ob, ctx_o, mod, out_norm_a[l], out_norm_b[l][perm], w_out_b)
        xs = _ffn_half(xs, mod, 2, norm_ffn2[l], w13b_2, w2b_2, l, BATCH if last else N_GROUPS,
                       final_g=norm_f if last else None)
    return xs
```

```python
import functools

import numpy as np
import jax
import jax.numpy as jnp
from jax import lax
from jax.experimental import pallas as pl
from jax.experimental.pallas import tpu as pltpu

F32 = jnp.float32
BF16 = jnp.bfloat16

D_MODEL = 1024
BATCH = 8
SEQ = 2048
DEPTH = 2
GRID_W = 64
GRID_H = SEQ // GRID_W
CTX_LEN = 256
HEAD_DIM = 64
NA_HEADS = 8
GQA_HEADS = 8
GQA_KV_HEADS = 2
GQA_GROUP = GQA_HEADS // GQA_KV_HEADS
NA_WIDTH = NA_HEADS * HEAD_DIM
GQA_WIDTH = GQA_HEADS * HEAD_DIM
GQA_KV_WIDTH = GQA_KV_HEADS * HEAD_DIM
IN_WIDTH = 3 * NA_WIDTH + GQA_WIDTH + 2 * GQA_KV_WIDTH
NA_WIN_H = 8
NA_WIN_W = 16
RPB_H = 2 * NA_WIN_H - 1
RPB_W = 2 * NA_WIN_W - 1
D_FF = 2816
N_MOD = 9
ROPE_THETA = 10000.0
EPS = 1e-6
NEG_INF = -1e30
LOG2_E = 1.4426950408889634
Q_SCALE = LOG2_E * HEAD_DIM ** -0.5

N_GROUPS = BATCH + 1
CTX_GROUP = BATCH
COND_ROWS = 16

V7X_LANES = 128
V7X_VMEM_BYTES = 64 * 1024 * 1024
V7X_VMEM_REQUEST_CAP = 56 * 1024 * 1024

ROW_TILE = 1024
SPLIT_ROW_TILE = 512
FF_CHUNK = 256
N_FF_CHUNKS = D_FF // FF_CHUNK
MOD_TILE = 1152
NA_Q_ROWS = 4
NA_Q_TILE = NA_Q_ROWS * GRID_W
NA_K_ROWS = NA_Q_ROWS + NA_WIN_H - 1
NA_K_TILE = NA_K_ROWS * GRID_W
NA_BLOCKS = GRID_H // NA_Q_ROWS
NA_TILES = BATCH * NA_BLOCKS
GQA_Q_TILE = 512
GQA_SUB_TILE = 64
GQA_TILES = BATCH * (SEQ // GQA_Q_TILE)
GQA_KEY_CHUNK = 256

assert BATCH * CTX_LEN == SEQ
assert D_FF % FF_CHUNK == 0 and SEQ % ROW_TILE == 0 and (N_MOD * D_MODEL) % MOD_TILE == 0
assert SEQ % SPLIT_ROW_TILE == 0
assert GRID_H % NA_Q_ROWS == 0 and SEQ % GQA_Q_TILE == 0 and GQA_Q_TILE % GQA_SUB_TILE == 0
assert SEQ % GQA_KEY_CHUNK == 0 and CTX_LEN % GQA_KEY_CHUNK == 0


def _vmem_limit(block_bytes, scratch_bytes, temp_bytes):
    need = 2 * block_bytes + scratch_bytes + temp_bytes
    return int(min(max(need, 16 * 1024 * 1024), V7X_VMEM_REQUEST_CAP))


def _hbm(*arrays):
    return [pltpu.with_memory_space_constraint(a, pltpu.HBM) for a in arrays]


def _hbm_out(shape, dtype):
    return pltpu.HBM(tuple(shape), dtype)


def _rms(x):
    return x * lax.rsqrt(jnp.mean(x * x, axis=-1, keepdims=True) + EPS)


def _dot(a, b):
    return jnp.dot(a, b, preferred_element_type=F32)


def _dot_nt(a, b):
    return lax.dot_general(a, b, (((1,), (1,)), ((), ())), preferred_element_type=F32)


def _mod_kernel(c_ref, w_ref, b_ref, o_ref):
    c = c_ref[...]
    s = (c * jax.nn.sigmoid(c)).astype(BF16)
    o_ref[0] = _dot(s, w_ref[0].astype(BF16)) + b_ref[0]


def _modulation(cond, w_mod, b_mod):
    n = N_MOD * D_MODEL
    return pl.pallas_call(
        _mod_kernel,
        grid=(DEPTH, n // MOD_TILE),
        in_specs=[
            pl.BlockSpec((COND_ROWS, D_MODEL), lambda l, j: (0, 0)),
            pl.BlockSpec((1, D_MODEL, MOD_TILE), lambda l, j: (l, 0, j)),
            pl.BlockSpec((1, 1, MOD_TILE), lambda l, j: (l, 0, j)),
        ],
        out_specs=pl.BlockSpec((1, COND_ROWS, MOD_TILE), lambda l, j: (l, 0, j)),
        out_shape=_hbm_out((DEPTH, COND_ROWS, n), F32),
        compiler_params=pltpu.CompilerParams(
            dimension_semantics=("parallel", "parallel"),
            vmem_limit_bytes=_vmem_limit(D_MODEL * MOD_TILE * 4, 0, D_MODEL * MOD_TILE * 2)),
        name="adaln_modulation",
    )(*_hbm(cond, w_mod, b_mod.reshape(DEPTH, 1, n)))


CAST_BLOCK_BYTES = 6 * 1024 * 1024


def _cast_kernel(w_ref, o_ref):
    o_ref[...] = w_ref[...].astype(BF16)


def _cast_bf16(w):
    layers, rows, cols = w.shape
    tile = rows
    while tile * cols * 4 > CAST_BLOCK_BYTES and tile % 32 == 0:
        tile //= 2
    spec = pl.BlockSpec((1, tile, cols), lambda l, i: (l, i, 0))
    return pl.pallas_call(
        _cast_kernel,
        grid=(layers, rows // tile),
        in_specs=[spec],
        out_specs=spec,
        out_shape=_hbm_out(w.shape, BF16),
        compiler_params=pltpu.CompilerParams(
            dimension_semantics=("parallel", "parallel"),
            vmem_limit_bytes=_vmem_limit(tile * cols * 6, 0, tile * cols * 2)),
        name="cast_bf16",
    )(*_hbm(w))


def _ffn_kernel(*refs, final, split_ctx):
    if split_ctx:
        x_ref, ctx_ref, mod_ref, g_ref, w13_ref, w2_ref, gf_ref, o_ref, u_ref = refs
        x = jnp.where(pl.program_id(0) == CTX_GROUP, ctx_ref[0], x_ref[0])
    else:
        x_ref, mod_ref, g_ref, w13_ref, w2_ref, gf_ref, o_ref, u_ref = refs
        x = x_ref[0]
    m = mod_ref[0]
    shift, scale, gate = m[:, :D_MODEL], m[:, D_MODEL:2 * D_MODEL], m[:, 2 * D_MODEL:]
    h = ((_rms(x) * g_ref[...]) * (1.0 + scale) + shift).astype(BF16)
    for j in range(N_FF_CHUNKS):
        a = _dot(h, w13_ref[0, :, j * FF_CHUNK:(j + 1) * FF_CHUNK])
        g = _dot(h, w13_ref[0, :, D_FF + j * FF_CHUNK:D_FF + (j + 1) * FF_CHUNK])
        u_ref[:, j * FF_CHUNK:(j + 1) * FF_CHUNK] = ((g * jax.nn.sigmoid(g)) * a).astype(BF16)
    y = _dot(u_ref[...], w2_ref[0])
    out = x + (0.5 * gate) * y
    if final:
        out = _rms(out) * gf_ref[...]
    o_ref[0] = out


def _ffn_half(x, mod, sub, norm_g, w13b, w2b, layer, n_groups, ctx=None, final_g=None):
    final = final_g is not None
    split_ctx = ctx is not None
    gf = final_g if final else norm_g
    tile = SPLIT_ROW_TILE if split_ctx else ROW_TILE
    blk = tile * D_MODEL * 4
    wbytes = 3 * D_FF * D_MODEL * 2
    last_tile = SEQ // tile - 1
    if split_ctx:
        x_specs = [
            pl.BlockSpec((1, tile, D_MODEL),
                         lambda g, i: (jnp.minimum(g, BATCH - 1), jnp.where(g == CTX_GROUP, last_tile, i), 0)),
            pl.BlockSpec((1, tile, D_MODEL), lambda g, i: (0, jnp.where(g == CTX_GROUP, i, 0), 0)),
        ]
        xs = (x, ctx)
    else:
        x_specs = [pl.BlockSpec((1, tile, D_MODEL), lambda g, i: (g, i, 0))]
        xs = (x,)
    return pl.pallas_call(
        functools.partial(_ffn_kernel, final=final, split_ctx=split_ctx),
        grid=(n_groups, SEQ // tile),
        in_specs=x_specs + [
            pl.BlockSpec((1, 1, 3 * D_MODEL), lambda g, i: (g, 0, sub)),
            pl.BlockSpec((1, D_MODEL), lambda g, i: (0, 0)),
            pl.BlockSpec((1, D_MODEL, 2 * D_FF), lambda g, i: (layer, 0, 0), pipeline_mode=pl.Buffered(1)),
            pl.BlockSpec((1, D_FF, D_MODEL), lambda g, i: (layer, 0, 0), pipeline_mode=pl.Buffered(1)),
            pl.BlockSpec((1, D_MODEL), lambda g, i: (0, 0)),
        ],
        out_specs=pl.BlockSpec((1, tile, D_MODEL), lambda g, i: (g, i, 0)),
        out_shape=_hbm_out((n_groups, SEQ, D_MODEL), F32),
        scratch_shapes=[pltpu.VMEM((tile, D_FF), BF16)],
        compiler_params=pltpu.CompilerParams(
            dimension_semantics=("arbitrary", "arbitrary"),
            vmem_limit_bytes=_vmem_limit((2 + split_ctx) * blk, wbytes + tile * D_FF * 2, 3 * blk)),
        name="ffn_half_final" if final else ("ffn_half_first" if split_ctx else "ffn_half"),
    )(*_hbm(*xs, mod, norm_g.reshape(1, D_MODEL), w13b, w2b, gf.reshape(1, D_MODEL)))


def _norm_rope(x, ones2, gain, cos, sin, bit16):
    sq = x * x
    hi = sq.astype(BF16)
    lo = (sq - hi.astype(F32)).astype(BF16)
    ssum = _dot(jnp.concatenate([hi, lo], axis=1), ones2)
    y = (x * lax.rsqrt(ssum * (1.0 / HEAD_DIM) + EPS)) * gain
    quarter = HEAD_DIM // 4
    partner = jnp.where(bit16, pltpu.roll(y, quarter, 1), pltpu.roll(y, V7X_LANES - quarter, 1))
    return y * cos + partner * sin


def _proj_kernel(x_ref, mod_ref, g_ref, w_ref, ones_ref, gq_ref, gk_ref, cos_ref, sin_ref,
                 qa_ref, ka_ref, va_ref, qb_ref, kb_ref, vb_ref):
    x = x_ref[0]
    m = mod_ref[0]
    shift, scale = m[:, :D_MODEL], m[:, D_MODEL:2 * D_MODEL]
    h = ((_rms(x) * g_ref[...]) * (1.0 + scale) + shift).astype(BF16)
    qb0 = 3 * NA_WIDTH
    kb0 = qb0 + GQA_WIDTH
    cos = cos_ref[0]
    sin = sin_ref[0]
    ones2 = ones_ref[...]
    lane = lax.broadcasted_iota(jnp.int32, (ROW_TILE, V7X_LANES), 1)
    bit16 = (lane & (HEAD_DIM // 4)) != 0
    gq = gq_ref[...] * Q_SCALE
    gk = gk_ref[...]
    q = _dot(h, w_ref[:, qb0:kb0])
    for c in range(GQA_WIDTH // V7X_LANES):
        sl = slice(c * V7X_LANES, (c + 1) * V7X_LANES)
        qb_ref[0, :, sl] = _norm_rope(q[:, sl], ones2, gq, cos, sin, bit16).astype(BF16)
    kv = _dot(h, w_ref[:, kb0:kb0 + 2 * GQA_KV_WIDTH])
    kb_ref[0] = _norm_rope(kv[:, :GQA_KV_WIDTH], ones2, gk, cos, sin, bit16).astype(BF16)
    vb_ref[0] = kv[:, GQA_KV_WIDTH:].astype(BF16)
    qa_ref[0] = (_dot(h, w_ref[:, 0:NA_WIDTH]) * Q_SCALE).astype(BF16)
    ka_ref[0] = _dot(h, w_ref[:, NA_WIDTH:2 * NA_WIDTH]).astype(BF16)
    va_ref[0] = _dot(h, w_ref[:, 2 * NA_WIDTH:3 * NA_WIDTH]).astype(BF16)


def _in_projection(x, mod, norm_g, w_in_b, ones2, gq, gk, cos_t, sin_t):
    blk_in = ROW_TILE * D_MODEL * 4
    blk_out = ROW_TILE * IN_WIDTH * 2
    row = lambda g, i: (g, i, 0)
    tab = lambda g, i: (jnp.where(g == CTX_GROUP, 1, 0), i, 0)
    const2 = lambda g, i: (0, 0)
    widths = (NA_WIDTH, NA_WIDTH, NA_WIDTH, GQA_WIDTH, GQA_KV_WIDTH, GQA_KV_WIDTH)
    return pl.pallas_call(
        _proj_kernel,
        grid=(N_GROUPS, SEQ // ROW_TILE),
        in_specs=[
            pl.BlockSpec((1, ROW_TILE, D_MODEL), row),
            pl.BlockSpec((1, 1, 3 * D_MODEL), lambda g, i: (g, 0, 1)),
            pl.BlockSpec((1, D_MODEL), const2),
            pl.BlockSpec((D_MODEL, IN_WIDTH), const2, pipeline_mode=pl.Buffered(1)),
            pl.BlockSpec((2 * V7X_LANES, V7X_LANES), const2),
            pl.BlockSpec((1, V7X_LANES), const2),
            pl.BlockSpec((1, V7X_LANES), const2),
            pl.BlockSpec((1, ROW_TILE, V7X_LANES), tab),
            pl.BlockSpec((1, ROW_TILE, V7X_LANES), tab),
        ],
        out_specs=[pl.BlockSpec((1, ROW_TILE, w), row) for w in widths],
        out_shape=[_hbm_out((N_GROUPS, SEQ, w), BF16) for w in widths],
        compiler_params=pltpu.CompilerParams(
            dimension_semantics=("parallel", "parallel"),
            vmem_limit_bytes=_vmem_limit(blk_in + blk_out, D_MODEL * IN_WIDTH * 2, 3 * blk_in)),
        name="in_projection",
    )(*_hbm(x, mod, norm_g.reshape(1, D_MODEL), w_in_b, ones2, gq, gk, cos_t, sin_t))


def _with_ones(v, hh):
    low = _low_lanes(v.shape[0])
    one = jnp.ones_like(v)
    return jnp.where(low, v, one) if hh == 0 else jnp.where(low, one, v)


def _normalise(o):
    return o / pltpu.roll(o, HEAD_DIM, 1)


def _softmax_pv(s, v, hh):
    e = jnp.exp2(s - s.max(axis=-1, keepdims=True)).astype(BF16)
    return _normalise(_dot(e, _with_ones(v, hh)))


def _low_lanes(n):
    return lax.broadcasted_iota(jnp.int32, (n, V7X_LANES), 1) < HEAD_DIM


def _mask_head(q, hh):
    low = _low_lanes(q.shape[0])
    zero = jnp.zeros_like(q)
    return jnp.where(low, q, zero) if hh == 0 else jnp.where(low, zero, q)


def _two_stage_steps(j, init, step):
    @pl.when(j == 0)
    def _():
        init(1)

    @pl.when(j % 2 == 0)
    def _():
        step(0, 1)

    @pl.when(j % 2 == 1)
    def _():
        step(1, 0)


def _na_first_key(i):
    first_key_row = jnp.clip(NA_Q_ROWS * i - NA_WIN_H // 2, 0, GRID_H - NA_K_ROWS)
    return pl.multiple_of(first_key_row * GRID_W, GRID_W)


def _na_kernel(q_ref, k_ref, kc_ref, bias_ref, v_ref, vc_ref, o_ref, e_ref, ec_ref):
    j = pl.program_id(0)
    k0_a = _na_first_key(jnp.minimum(j, NA_TILES - 1) % NA_BLOCKS)
    k0_b = _na_first_key(jnp.maximum(j - 1, 0) % NA_BLOCKS)

    def init(slot):
        e_ref[slot] = jnp.ones(e_ref.shape[1:], BF16)
        ec_ref[slot] = jnp.ones(ec_ref.shape[1:], BF16)

    def store_exp(slot, h, s_loc, s_ctx, m):
        e_ref[slot, h] = jnp.exp2(s_loc - m).astype(BF16)
        ec_ref[slot, h] = jnp.exp2(s_ctx - m).astype(BF16)

    def step(slot_a, slot_b):
        low = _low_lanes(NA_Q_TILE)
        pending = None
        o_pair = []
        for h in range(NA_HEADS):
            c, hh = divmod(h, 2)
            sl = slice(c * V7X_LANES, (c + 1) * V7X_LANES)
            qm = _mask_head(q_ref[0, :, sl], hh)
            s_loc = _dot_nt(qm, k_ref[0, pl.ds(k0_a, NA_K_TILE), sl]) + bias_ref[0, 0, h]
            s_ctx = _dot_nt(qm, kc_ref[0, :, sl])
            m = jnp.maximum(s_loc.max(axis=-1, keepdims=True), s_ctx.max(axis=-1, keepdims=True))
            if pending is not None:
                store_exp(slot_a, *pending)
            pending = (h, s_loc, s_ctx, m)
            vw = _with_ones(v_ref[0, pl.ds(k0_b, NA_K_TILE), sl], hh)
            vc = _with_ones(vc_ref[0, :, sl], hh)
            o_pair.append(_normalise(_dot(e_ref[slot_b, h], vw) + _dot(ec_ref[slot_b, h], vc)))
            if hh == 1:
                o_ref[0, :, sl] = jnp.where(low, o_pair[0], o_pair[1])
                o_pair = []
        store_exp(slot_a, *pending)

    _two_stage_steps(j, init, step)


def _na_block_picks():
    picks = np.full((3, NA_Q_ROWS, NA_K_ROWS), RPB_H, np.int32)
    for v, i in enumerate((0, 1, NA_BLOCKS - 1)):
        first_key_row = min(max(NA_Q_ROWS * i - NA_WIN_H // 2, 0), GRID_H - NA_K_ROWS)
        for a in range(NA_Q_ROWS):
            qr = NA_Q_ROWS * i + a
            rs = min(max(qr - NA_WIN_H // 2, 0), GRID_H - NA_WIN_H)
            for n in range(NA_K_ROWS):
                kr = first_key_row + n
                if rs <= kr < rs + NA_WIN_H:
                    picks[v, a, n] = kr - qr + NA_WIN_H - 1
    return picks


def _na_bias_kernel(ring_ref, o_ref, *, picks):
    ring = ring_ref[0, 0]
    qc = lax.broadcasted_iota(jnp.int32, (GRID_W, V7X_LANES), 0)
    kc = lax.broadcasted_iota(jnp.int32, (GRID_W, V7X_LANES), 1)
    cs = jnp.clip(qc - NA_WIN_W // 2, 0, GRID_W - NA_WIN_W)
    in_win = (kc >= cs) & (kc < cs + NA_WIN_W)
    blocks = []
    for r in range(RPB_H):
        rows = jnp.broadcast_to(ring[r:r + 1, :], (GRID_W, V7X_LANES))
        toe = pltpu.roll(rows, 0, 1, stride=1, stride_axis=0)
        blocks.append(jnp.where(in_win, toe * LOG2_E, NEG_INF)[:, :GRID_W])
    blocks.append(jnp.full((GRID_W, GRID_W), NEG_INF, F32))
    for v in range(picks.shape[0]):
        for a in range(NA_Q_ROWS):
            o_ref[0, v, 0, a * GRID_W:(a + 1) * GRID_W, :] = jnp.concatenate(
                [blocks[p] for p in picks[v, a]], axis=1)


def _na_bias_table(rpb):
    layers = rpb.shape[0]
    pad_rows = 16 - RPB_H
    ring = jnp.concatenate(
        [rpb[..., NA_WIN_W - 1:], jnp.zeros(rpb.shape[:3] + (V7X_LANES - RPB_W,), F32),
         rpb[..., :NA_WIN_W - 1]], axis=-1)
    ring = jnp.pad(ring, ((0, 0), (0, 0), (0, pad_rows), (0, 0)))
    return pl.pallas_call(
        functools.partial(_na_bias_kernel, picks=_na_block_picks()),
        grid=(layers, NA_HEADS),
        in_specs=[pl.BlockSpec((1, 1, RPB_H + pad_rows, V7X_LANES), lambda l, h: (l, h, 0, 0))],
        out_specs=pl.BlockSpec((1, 3, 1, NA_Q_TILE, NA_K_TILE), lambda l, h: (l, 0, h, 0, 0)),
        out_shape=_hbm_out((layers, 3, NA_HEADS, NA_Q_TILE, NA_K_TILE), F32),
        compiler_params=pltpu.CompilerParams(dimension_semantics=("parallel", "parallel")),
        name="na_bias_table",
    )(*_hbm(ring))


def _na_attention(qa, ka, va, bias, layer):
    tile_a = lambda j: jnp.minimum(j, NA_TILES - 1)
    tile_b = lambda j: jnp.maximum(j - 1, 0)

    def variant(j):
        i = tile_a(j) % NA_BLOCKS
        return (layer, jnp.where(i == 0, 0, jnp.where(i == NA_BLOCKS - 1, 2, 1)), 0, 0, 0)

    blk = (NA_Q_TILE * NA_WIDTH * (2 + 4) + 2 * (SEQ + CTX_LEN) * NA_WIDTH * 2
           + NA_HEADS * NA_Q_TILE * NA_K_TILE * 4)
    scratch = 2 * NA_HEADS * NA_Q_TILE * (NA_K_TILE + CTX_LEN) * 2
    scores = NA_HEADS * NA_Q_TILE * (NA_K_TILE + CTX_LEN) * 4
    return pl.pallas_call(
        _na_kernel,
        grid=(NA_TILES + 1,),
        in_specs=[
            pl.BlockSpec((1, NA_Q_TILE, NA_WIDTH), lambda j: (tile_a(j) // NA_BLOCKS, tile_a(j) % NA_BLOCKS, 0)),
            pl.BlockSpec((1, SEQ, NA_WIDTH), lambda j: (tile_a(j) // NA_BLOCKS, 0, 0)),
            pl.BlockSpec((1, CTX_LEN, NA_WIDTH), lambda j: (CTX_GROUP, tile_a(j) // NA_BLOCKS, 0)),
            pl.BlockSpec((1, 1, NA_HEADS, NA_Q_TILE, NA_K_TILE), variant),
            pl.BlockSpec((1, SEQ, NA_WIDTH), lambda j: (tile_b(j) // NA_BLOCKS, 0, 0)),
            pl.BlockSpec((1, CTX_LEN, NA_WIDTH), lambda j: (CTX_GROUP, tile_b(j) // NA_BLOCKS, 0)),
        ],
        out_specs=pl.BlockSpec((1, NA_Q_TILE, NA_WIDTH),
                               lambda j: (tile_b(j) // NA_BLOCKS, tile_b(j) % NA_BLOCKS, 0)),
        out_shape=_hbm_out((BATCH, SEQ, NA_WIDTH), F32),
        scratch_shapes=[pltpu.VMEM((2, NA_HEADS, NA_Q_TILE, NA_K_TILE), BF16),
                        pltpu.VMEM((2, NA_HEADS, NA_Q_TILE, CTX_LEN), BF16)],
        compiler_params=pltpu.CompilerParams(
            dimension_semantics=("arbitrary",),
            vmem_limit_bytes=_vmem_limit(blk, scratch, scores)),
        name="neighbourhood_attention",
    )(*_hbm(qa, ka, ka, bias, va, va))


def _gqa_stack(q, kvh):
    return jnp.concatenate(
        [_mask_head(q[:, c * V7X_LANES:(c + 1) * V7X_LANES], kvh) for c in range(GQA_GROUP)], axis=0)


def _gqa_unstack(o_kv0, o_kv1, n_q):
    low = _low_lanes(n_q)
    return [jnp.where(low, o_kv0[c * n_q:(c + 1) * n_q], o_kv1[c * n_q:(c + 1) * n_q])
            for c in range(GQA_GROUP)]


def _gqa_kernel(q_ref, kt_ref, kct_ref, v_ref, vc_ref, o_ref, e_ref):
    j = pl.program_id(0)
    chains = [(t, kvh) for t in range(GQA_Q_TILE // GQA_SUB_TILE) for kvh in range(GQA_KV_HEADS)]
    n_lat = SEQ // GQA_KEY_CHUNK
    n_chunks = n_lat + CTX_LEN // GQA_KEY_CHUNK

    def chunk(ref_lat, ref_ctx, c, axis):
        src, cc = (ref_lat, c) if c < n_lat else (ref_ctx, c - n_lat)
        keys = slice(cc * GQA_KEY_CHUNK, (cc + 1) * GQA_KEY_CHUNK)
        return src[keys] if axis == 0 else src[0, :, keys]

    def step(scores_of_tile_j, values_of_tile_before):
        if values_of_tile_before:
            v_ones = [(_with_ones(v_ref[0], kvh), _with_ones(vc_ref[0], kvh)) for kvh in range(GQA_KV_HEADS)]
        scores, row_max, outs = {}, {}, {}
        for i in range(len(chains) + scores_of_tile_j):
            live = i < len(chains)
            if live:
                t, kvh = chains[i]
                rows = slice(t * GQA_SUB_TILE, (t + 1) * GQA_SUB_TILE)
            if live and scores_of_tile_j:
                stack = _gqa_stack(q_ref[0, rows, :], kvh)
                scores[i], lane_max = [], None
            acc = None
            for c in range(n_chunks):
                keys = slice(c * GQA_KEY_CHUNK, (c + 1) * GQA_KEY_CHUNK)
                if live and scores_of_tile_j:
                    s = _dot(stack, chunk(kt_ref, kct_ref, c, 1))
                    scores[i].append(s)
                    for w in range(GQA_KEY_CHUNK // V7X_LANES):
                        part = s[:, w * V7X_LANES:(w + 1) * V7X_LANES]
                        lane_max = part if lane_max is None else jnp.maximum(lane_max, part)
                if i >= 1 and scores_of_tile_j:
                    e_ref[i - 1, :, keys] = jnp.exp2(scores[i - 1][c] - row_max[i - 1]).astype(BF16)
                if live and values_of_tile_before:
                    part = _dot(e_ref[i, :, keys], chunk(v_ones[kvh][0], v_ones[kvh][1], c, 0))
                    acc = part if acc is None else acc + part
            if i >= 1 and scores_of_tile_j:
                del scores[i - 1]
            if live and scores_of_tile_j:
                row_max[i] = lane_max.max(axis=-1, keepdims=True)
            if live and values_of_tile_before:
                outs[kvh] = _normalise(acc)
                if kvh == GQA_KV_HEADS - 1:
                    for col, oc in enumerate(_gqa_unstack(outs[0], outs[1], GQA_SUB_TILE)):
                        o_ref[0, rows, col * V7X_LANES:(col + 1) * V7X_LANES] = oc

    @pl.when(j == 0)
    def _():
        step(True, False)

    @pl.when((j > 0) & (j < GQA_TILES))
    def _():
        step(True, True)

    @pl.when(j == GQA_TILES)
    def _():
        step(False, True)


def _gqa_attention(qb, kbt, vb):
    tile_a = lambda j: jnp.minimum(j, GQA_TILES - 1)
    tile_b = lambda j: jnp.maximum(j - 1, 0)
    per_b = SEQ // GQA_Q_TILE
    chains = GQA_KV_HEADS * GQA_Q_TILE // GQA_SUB_TILE
    chain_rows = GQA_GROUP * GQA_SUB_TILE
    blk = (GQA_Q_TILE * GQA_WIDTH * (2 + 4) + 2 * (SEQ + CTX_LEN) * V7X_LANES * 2)
    scratch = chains * chain_rows * (SEQ + CTX_LEN) * 2
    scores = chain_rows * (SEQ + CTX_LEN) * 4
    return pl.pallas_call(
        _gqa_kernel,
        grid=(GQA_TILES + 1,),
        in_specs=[
            pl.BlockSpec((1, GQA_Q_TILE, GQA_WIDTH), lambda j: (tile_a(j) // per_b, tile_a(j) % per_b, 0)),
            pl.BlockSpec((1, GQA_KV_WIDTH, SEQ), lambda j: (tile_a(j) // per_b, 0, 0)),
            pl.BlockSpec((1, GQA_KV_WIDTH, CTX_LEN), lambda j: (CTX_GROUP, 0, tile_a(j) // per_b)),
            pl.BlockSpec((1, SEQ, GQA_KV_WIDTH), lambda j: (tile_b(j) // per_b, 0, 0)),
            pl.BlockSpec((1, CTX_LEN, GQA_KV_WIDTH), lambda j: (CTX_GROUP, tile_b(j) // per_b, 0)),
        ],
        out_specs=pl.BlockSpec((1, GQA_Q_TILE, GQA_WIDTH), lambda j: (tile_b(j) // per_b, tile_b(j) % per_b, 0)),
        out_shape=_hbm_out((BATCH, SEQ, GQA_WIDTH), F32),
        scratch_shapes=[pltpu.VMEM((chains, chain_rows, SEQ + CTX_LEN), BF16)],
        compiler_params=pltpu.CompilerParams(
            dimension_semantics=("arbitrary",),
            vmem_limit_bytes=_vmem_limit(blk, scratch, 4 * scores)),
        name="gqa_attention",
    )(*_hbm(qb, kbt, kbt, vb, vb))


def _ctx_kernel(qa_ref, ka_ref, va_ref, qb_ref, kbt_ref, vb_ref, oa_ref, ob_ref):
    low = _low_lanes(CTX_LEN)
    for c in range(NA_WIDTH // V7X_LANES):
        sl = slice(c * V7X_LANES, (c + 1) * V7X_LANES)
        q, k, v = qa_ref[0, :, sl], ka_ref[0, :, sl], va_ref[0, :, sl]
        o = [_softmax_pv(_dot_nt(_mask_head(q, hh), k), v, hh) for hh in range(2)]
        oa_ref[0, :, sl] = jnp.where(low, o[0], o[1])
    q = qb_ref[0]
    o = [_softmax_pv(_dot(_gqa_stack(q, kvh), kbt_ref[0]), vb_ref[0], kvh) for kvh in range(GQA_KV_HEADS)]
    for c, oc in enumerate(_gqa_unstack(o[0], o[1], CTX_LEN)):
        ob_ref[0, :, c * V7X_LANES:(c + 1) * V7X_LANES] = oc


def _ctx_attention(qa, ka, va, qb, kbt, vb):
    row = lambda b: (CTX_GROUP, b, 0)
    wide = pl.BlockSpec((1, CTX_LEN, NA_WIDTH), row)
    narrow = pl.BlockSpec((1, CTX_LEN, GQA_KV_WIDTH), row)
    narrow_t = pl.BlockSpec((1, GQA_KV_WIDTH, CTX_LEN), lambda b: (CTX_GROUP, 0, b))
    out = pl.BlockSpec((1, CTX_LEN, NA_WIDTH), lambda b: (0, b, 0))
    shape = _hbm_out((1, SEQ, NA_WIDTH), F32)
    blk = CTX_LEN * (4 * NA_WIDTH * 2 + 2 * GQA_KV_WIDTH * 2 + 2 * NA_WIDTH * 4)
    return pl.pallas_call(
        _ctx_kernel,
        grid=(BATCH,),
        in_specs=[wide, wide, wide, wide, narrow_t, narrow],
        out_specs=[out, out],
        out_shape=[shape, shape],
        compiler_params=pltpu.CompilerParams(
            dimension_semantics=("parallel",),
            vmem_limit_bytes=_vmem_limit(blk, 0, 16 * GQA_GROUP * CTX_LEN * CTX_LEN * 4)),
        name="context_attention",
    )(*_hbm(qa, ka, va, qb, kbt, vb))


def _outproj_kernel(*refs, split_ctx):
    if split_ctx:
        x_ref, oa_ref, ob_ref, oac_ref, obc_ref, mod_ref, ga_ref, gb_ref, w_ref, o_ref = refs
        is_ctx = pl.program_id(0) == CTX_GROUP
        oa = jnp.where(is_ctx, oac_ref[0], oa_ref[0])
        ob = jnp.where(is_ctx, obc_ref[0], ob_ref[0])
    else:
        x_ref, oa_ref, ob_ref, mod_ref, ga_ref, gb_ref, w_ref, o_ref = refs
        oa, ob = oa_ref[0], ob_ref[0]
    gate = mod_ref[0][:, 2 * D_MODEL:]
    ya = (_rms(oa) * ga_ref[...]).astype(BF16)
    yb = (_rms(ob) * gb_ref[...]).astype(BF16)
    y = _dot(ya, w_ref[:NA_WIDTH, :]) + _dot(yb, w_ref[NA_WIDTH:, :])
    o_ref[0] = x_ref[0] + gate * y


def _out_projection(x, oa, ob, ctx_o, mod, ga, gb, w_out_b):
    split_ctx = ctx_o is not None
    n_groups = N_GROUPS if split_ctx else BATCH
    row = lambda g, i: (g, i, 0)
    const2 = lambda g, i: (0, 0)
    last_tile = SEQ // ROW_TILE - 1
    lat = lambda g, i: (jnp.minimum(g, BATCH - 1), jnp.where(g == CTX_GROUP, last_tile, i), 0)
    cxt = lambda g, i: (0, jnp.where(g == CTX_GROUP, i, 0), 0)
    att = lambda imap: pl.BlockSpec((1, ROW_TILE, NA_WIDTH), imap)
    att_specs = [att(lat), att(lat), att(cxt), att(cxt)] if split_ctx else [att(row), att(row)]
    att_args = (oa, ob) + (tuple(ctx_o) if split_ctx else ())
    blk = ROW_TILE * (2 * D_MODEL + (2 + 2 * split_ctx) * NA_WIDTH) * 4
    return pl.pallas_call(
        functools.partial(_outproj_kernel, split_ctx=split_ctx),
        grid=(n_groups, SEQ // ROW_TILE),
        in_specs=[pl.BlockSpec((1, ROW_TILE, D_MODEL), row)] + att_specs + [
            pl.BlockSpec((1, 1, 3 * D_MODEL), lambda g, i: (g, 0, 1)),
            pl.BlockSpec((1, NA_WIDTH), const2),
            pl.BlockSpec((1, GQA_WIDTH), const2),
            pl.BlockSpec((NA_WIDTH + GQA_WIDTH, D_MODEL), const2, pipeline_mode=pl.Buffered(1)),
        ],
        out_specs=pl.BlockSpec((1, ROW_TILE, D_MODEL), row),
        out_shape=_hbm_out((n_groups, SEQ, D_MODEL), F32),
        compiler_params=pltpu.CompilerParams(
            dimension_semantics=("arbitrary", "arbitrary"),
            vmem_limit_bytes=_vmem_limit(blk, (NA_WIDTH + GQA_WIDTH) * D_MODEL * 2, blk)),
        name="out_projection_ctx" if split_ctx else "out_projection",
    )(*_hbm(x, *att_args, mod, ga.reshape(1, NA_WIDTH), gb.reshape(1, GQA_WIDTH), w_out_b))


def _gqa_head_permutation():
    p = np.arange(GQA_WIDTH)
    col, half, d = p // V7X_LANES, (p % V7X_LANES) // HEAD_DIM, p % HEAD_DIM
    return HEAD_DIM * (col + GQA_GROUP * half) + d


def _rope_tables():
    t = jnp.arange(SEQ, dtype=jnp.int32)
    row = (t // GRID_W).astype(F32)
    col = (t % GRID_W).astype(F32)
    axis_dim = HEAD_DIM // 2
    inv_freq = ROPE_THETA ** (-jnp.arange(0, axis_dim, 2, dtype=F32) / axis_dim)
    ang_r = row[:, None] * inv_freq[None, :]
    ang_c = col[:, None] * inv_freq[None, :]
    cos_r, sin_r, cos_c, sin_c = jnp.cos(ang_r), jnp.sin(ang_r), jnp.cos(ang_c), jnp.sin(ang_c)
    cos_h = jnp.concatenate([cos_r, cos_r, cos_c, cos_c], axis=-1)
    sin_h = jnp.concatenate([-sin_r, sin_r, -sin_c, sin_c], axis=-1)
    heads = V7X_LANES // HEAD_DIM
    cos2 = jnp.tile(cos_h, (1, heads))
    sin2 = jnp.tile(sin_h, (1, heads))
    return (jnp.stack([cos2, jnp.ones_like(cos2)]), jnp.stack([sin2, jnp.zeros_like(sin2)]))


def kernel(x, c, ctx, c_ctx, w_mod, b_mod, norm_ffn1, ffn1_w13, ffn1_w2, norm_mix, w_in, na_rpb,
           gqa_q_norm, gqa_k_norm, out_norm_a, out_norm_b, w_out, norm_ffn2, ffn2_w13, ffn2_w2, norm_f):
    perm = _gqa_head_permutation()
    cos_t, sin_t = _rope_tables()
    head_of_lane = np.arange(V7X_LANES) // HEAD_DIM
    ones_blk = np.asarray(head_of_lane[:, None] == head_of_lane[None, :], np.float32)
    ones2 = jnp.asarray(np.concatenate([ones_blk, ones_blk], axis=0), BF16)
    heads_per_col = V7X_LANES // HEAD_DIM
    bias_all = _na_bias_table(na_rpb)

    cond = jnp.zeros((COND_ROWS, D_MODEL), F32).at[:BATCH].set(c).at[CTX_GROUP].set(c_ctx)
    mod_all = _modulation(cond, w_mod, b_mod)

    w13b_1, w2b_1 = _cast_bf16(ffn1_w13), _cast_bf16(ffn1_w2)
    w13b_2, w2b_2 = _cast_bf16(ffn2_w13), _cast_bf16(ffn2_w2)

    xs = x
    for l in range(DEPTH):
        last = l == DEPTH - 1
        mod = mod_all[l].reshape(COND_ROWS, 1, N_MOD * D_MODEL)
        qb0 = 3 * NA_WIDTH
        w_in_l = w_in[l]
        w_in_b = jnp.concatenate(
            [w_in_l[:, :qb0], w_in_l[:, qb0:qb0 + GQA_WIDTH][:, perm], w_in_l[:, qb0 + GQA_WIDTH:]],
            axis=1).astype(BF16)
        w_out_l = w_out[l]
        w_out_b = jnp.concatenate([w_out_l[:NA_WIDTH], w_out_l[NA_WIDTH:][perm]], axis=0).astype(BF16)
        gq = jnp.tile(gqa_q_norm[l], heads_per_col).reshape(1, V7X_LANES)
        gk = jnp.tile(gqa_k_norm[l], heads_per_col).reshape(1, V7X_LANES)

        xs = _ffn_half(xs, mod, 0, norm_ffn1[l], w13b_1, w2b_1, l, N_GROUPS,
                       ctx=ctx.reshape(1, SEQ, D_MODEL) if l == 0 else None)
        qa, ka, va, qb, kb, vb = _in_projection(xs, mod, norm_mix[l], w_in_b, ones2, gq, gk,
                                                cos_t, sin_t)
        kbt = jnp.swapaxes(kb, 1, 2)
        oa = _na_attention(qa, ka, va, bias_all, l)
        ob = _gqa_attention(qb, kbt, vb)
        ctx_o = None if last else _ctx_attention(qa, ka, va, qb, kbt, vb)
        xs = _out_projection(xs, oa, ob, ctx_o, mod, out_norm_a[l], out_norm_b[l][perm], w_out_b)
        xs = _ffn_half(xs, mod, 2, norm_ffn2[l], w13b_2, w2b_2, l, BATCH if last else N_GROUPS,
                       final_g=norm_f if last else None)
    return xs
```

```python
import functools

import numpy as np
import jax
import jax.numpy as jnp
from jax import lax
from jax.experimental import pallas as pl
from jax.experimental.pallas import tpu as pltpu

F32 = jnp.float32
BF16 = jnp.bfloat16

D_MODEL = 1024
BATCH = 8
SEQ = 2048
DEPTH = 2
GRID_W = 64
GRID_H = SEQ // GRID_W
CTX_LEN = 256
HEAD_DIM = 64
NA_HEADS = 8
GQA_HEADS = 8
GQA_KV_HEADS = 2
GQA_GROUP = GQA_HEADS // GQA_KV_HEADS
NA_WIDTH = NA_HEADS * HEAD_DIM
GQA_WIDTH = GQA_HEADS * HEAD_DIM
GQA_KV_WIDTH = GQA_KV_HEADS * HEAD_DIM
IN_WIDTH = 3 * NA_WIDTH + GQA_WIDTH + 2 * GQA_KV_WIDTH
NA_WIN_H = 8
NA_WIN_W = 16
RPB_H = 2 * NA_WIN_H - 1
RPB_W = 2 * NA_WIN_W - 1
D_FF = 2816
N_MOD = 9
ROPE_THETA = 10000.0
EPS = 1e-6
NEG_INF = -1e30
LOG2_E = 1.4426950408889634
Q_SCALE = LOG2_E * HEAD_DIM ** -0.5

N_GROUPS = BATCH + 1
CTX_GROUP = BATCH
COND_ROWS = 16

V7X_LANES = 128
V7X_VMEM_BYTES = 64 * 1024 * 1024
V7X_VMEM_REQUEST_CAP = 56 * 1024 * 1024

ROW_TILE = 1024
SPLIT_ROW_TILE = 512
FF_CHUNK = 256
N_FF_CHUNKS = D_FF // FF_CHUNK
MOD_TILE = 1152
NA_Q_ROWS = 4
NA_Q_TILE = NA_Q_ROWS * GRID_W
NA_K_ROWS = NA_Q_ROWS + NA_WIN_H - 1
NA_K_TILE = NA_K_ROWS * GRID_W
NA_BLOCKS = GRID_H // NA_Q_ROWS
CTX_STEP_BATCHES = 2
NA_STEP_TILES = 2
NA_STEPS = BATCH * NA_BLOCKS // NA_STEP_TILES
GQA_Q_TILE = 512
GQA_SUB_TILE = 64
GQA_TILES = BATCH * (SEQ // GQA_Q_TILE)
GQA_KEY_CHUNK = 256

assert BATCH * CTX_LEN == SEQ
assert D_FF % FF_CHUNK == 0 and SEQ % ROW_TILE == 0 and (N_MOD * D_MODEL) % MOD_TILE == 0
assert SEQ % SPLIT_ROW_TILE == 0
assert GRID_H % NA_Q_ROWS == 0 and SEQ % GQA_Q_TILE == 0 and GQA_Q_TILE % GQA_SUB_TILE == 0
assert SEQ % GQA_KEY_CHUNK == 0 and CTX_LEN % GQA_KEY_CHUNK == 0
assert NA_BLOCKS % NA_STEP_TILES == 0 and NA_STEP_TILES == 2


def _vmem_limit(block_bytes, scratch_bytes, temp_bytes):
    need = 2 * block_bytes + scratch_bytes + temp_bytes
    return int(min(max(need, 16 * 1024 * 1024), V7X_VMEM_REQUEST_CAP))


def _rms(x):
    return x * lax.rsqrt(jnp.mean(x * x, axis=-1, keepdims=True) + EPS)


def _dot(a, b):
    return jnp.dot(a, b, preferred_element_type=F32)


def _dot_nt(a, b):
    return lax.dot_general(a, b, (((1,), (1,)), ((), ())), preferred_element_type=F32)


def _mod_kernel(c_ref, w_ref, b_ref, o_ref):
    c = c_ref[...]
    s = (c * jax.nn.sigmoid(c)).astype(BF16)
    o_ref[0] = _dot(s, w_ref[0].astype(BF16)) + b_ref[0]


def _modulation(cond, w_mod, b_mod):
    n = N_MOD * D_MODEL
    return pl.pallas_call(
        _mod_kernel,
        grid=(DEPTH, n // MOD_TILE),
        in_specs=[
            pl.BlockSpec((COND_ROWS, D_MODEL), lambda l, j: (0, 0)),
            pl.BlockSpec((1, D_MODEL, MOD_TILE), lambda l, j: (l, 0, j)),
            pl.BlockSpec((1, 1, MOD_TILE), lambda l, j: (l, 0, j)),
        ],
        out_specs=pl.BlockSpec((1, COND_ROWS, MOD_TILE), lambda l, j: (l, 0, j)),
        out_shape=jax.ShapeDtypeStruct((DEPTH, COND_ROWS, n), F32),
        compiler_params=pltpu.CompilerParams(
            dimension_semantics=("parallel", "parallel"),
            vmem_limit_bytes=_vmem_limit(D_MODEL * MOD_TILE * 4, 0, D_MODEL * MOD_TILE * 2)),
        name="adaln_modulation",
    )(cond, w_mod, b_mod.reshape(DEPTH, 1, n))


CAST_BLOCK_BYTES = 6 * 1024 * 1024


def _cast_kernel(w_ref, o_ref):
    o_ref[...] = w_ref[...].astype(BF16)


def _cast_bf16(w):
    layers, rows, cols = w.shape
    tile = rows
    while tile * cols * 4 > CAST_BLOCK_BYTES and tile % 32 == 0:
        tile //= 2
    spec = pl.BlockSpec((1, tile, cols), lambda l, i: (l, i, 0))
    return pl.pallas_call(
        _cast_kernel,
        grid=(layers, rows // tile),
        in_specs=[spec],
        out_specs=spec,
        out_shape=jax.ShapeDtypeStruct(w.shape, BF16),
        compiler_params=pltpu.CompilerParams(
            dimension_semantics=("parallel", "parallel"),
            vmem_limit_bytes=_vmem_limit(tile * cols * 6, 0, tile * cols * 2)),
        name="cast_bf16",
    )(w)


def _ffn_kernel(*refs, final, split_ctx):
    if split_ctx:
        x_ref, ctx_ref, mod_ref, g_ref, w13_ref, w2_ref, gf_ref, o_ref, u_ref = refs
        x = jnp.where(pl.program_id(0) == CTX_GROUP, ctx_ref[0], x_ref[0])
    else:
        x_ref, mod_ref, g_ref, w13_ref, w2_ref, gf_ref, o_ref, u_ref = refs
        x = x_ref[0]
    m = mod_ref[0]
    shift, scale, gate = m[:, :D_MODEL], m[:, D_MODEL:2 * D_MODEL], m[:, 2 * D_MODEL:]
    h = ((_rms(x) * g_ref[...]) * (1.0 + scale) + shift).astype(BF16)
    for j in range(N_FF_CHUNKS):
        a = _dot(h, w13_ref[0, :, j * FF_CHUNK:(j + 1) * FF_CHUNK])
        g = _dot(h, w13_ref[0, :, D_FF + j * FF_CHUNK:D_FF + (j + 1) * FF_CHUNK])
        u_ref[:, j * FF_CHUNK:(j + 1) * FF_CHUNK] = ((g * jax.nn.sigmoid(g)) * a).astype(BF16)
    y = _dot(u_ref[...], w2_ref[0])
    out = x + (0.5 * gate) * y
    if final:
        out = _rms(out) * gf_ref[...]
    o_ref[0] = out


def _ffn_half(x, mod, sub, norm_g, w13b, w2b, layer, n_groups, ctx=None, final_g=None):
    final = final_g is not None
    split_ctx = ctx is not None
    gf = final_g if final else norm_g
    tile = SPLIT_ROW_TILE if split_ctx else ROW_TILE
    blk = tile * D_MODEL * 4
    wbytes = 3 * D_FF * D_MODEL * 2
    last_tile = SEQ // tile - 1
    if split_ctx:
        x_specs = [
            pl.BlockSpec((1, tile, D_MODEL),
                         lambda g, i: (jnp.minimum(g, BATCH - 1), jnp.where(g == CTX_GROUP, last_tile, i), 0)),
            pl.BlockSpec((1, tile, D_MODEL), lambda g, i: (0, jnp.where(g == CTX_GROUP, i, 0), 0)),
        ]
        xs = (x, ctx)
    else:
        x_specs = [pl.BlockSpec((1, tile, D_MODEL), lambda g, i: (g, i, 0))]
        xs = (x,)
    return pl.pallas_call(
        functools.partial(_ffn_kernel, final=final, split_ctx=split_ctx),
        grid=(n_groups, SEQ // tile),
        in_specs=x_specs + [
            pl.BlockSpec((1, 1, 3 * D_MODEL), lambda g, i: (g, 0, sub)),
            pl.BlockSpec((1, D_MODEL), lambda g, i: (0, 0)),
            pl.BlockSpec((1, D_MODEL, 2 * D_FF), lambda g, i: (layer, 0, 0), pipeline_mode=pl.Buffered(1)),
            pl.BlockSpec((1, D_FF, D_MODEL), lambda g, i: (layer, 0, 0), pipeline_mode=pl.Buffered(1)),
            pl.BlockSpec((1, D_MODEL), lambda g, i: (0, 0)),
        ],
        out_specs=pl.BlockSpec((1, tile, D_MODEL), lambda g, i: (g, i, 0)),
        out_shape=jax.ShapeDtypeStruct((n_groups, SEQ, D_MODEL), F32),
        scratch_shapes=[pltpu.VMEM((tile, D_FF), BF16)],
        compiler_params=pltpu.CompilerParams(
            dimension_semantics=("arbitrary", "arbitrary"),
            vmem_limit_bytes=_vmem_limit((2 + split_ctx) * blk, wbytes + tile * D_FF * 2, 3 * blk)),
        name="ffn_half_final" if final else ("ffn_half_first" if split_ctx else "ffn_half"),
    )(*xs, mod, norm_g.reshape(1, D_MODEL), w13b, w2b, gf.reshape(1, D_MODEL))


def _norm_rope(x, ones2, gain, cos, sin, bit16):
    sq = x * x
    hi = sq.astype(BF16)
    lo = (sq - hi.astype(F32)).astype(BF16)
    ssum = _dot(jnp.concatenate([hi, lo], axis=1), ones2)
    y = (x * lax.rsqrt(ssum * (1.0 / HEAD_DIM) + EPS)) * gain
    quarter = HEAD_DIM // 4
    partner = jnp.where(bit16, pltpu.roll(y, quarter, 1), pltpu.roll(y, V7X_LANES - quarter, 1))
    return y * cos + partner * sin


def _proj_kernel(x_ref, mod_ref, g_ref, w_ref, ones_ref, gq_ref, gk_ref, cos_ref, sin_ref,
                 qa_ref, ka_ref, va_ref, qb_ref, kb_ref, vb_ref):
    x = x_ref[0]
    m = mod_ref[0]
    shift, scale = m[:, :D_MODEL], m[:, D_MODEL:2 * D_MODEL]
    h = ((_rms(x) * g_ref[...]) * (1.0 + scale) + shift).astype(BF16)
    qb0 = 3 * NA_WIDTH
    kb0 = qb0 + GQA_WIDTH
    cos = cos_ref[0]
    sin = sin_ref[0]
    ones2 = ones_ref[...]
    lane = lax.broadcasted_iota(jnp.int32, (ROW_TILE, V7X_LANES), 1)
    bit16 = (lane & (HEAD_DIM // 4)) != 0
    gq = gq_ref[...] * Q_SCALE
    gk = gk_ref[...]
    q = _dot(h, w_ref[:, qb0:kb0])
    for c in range(GQA_WIDTH // V7X_LANES):
        sl = slice(c * V7X_LANES, (c + 1) * V7X_LANES)
        qb_ref[0, :, sl] = _norm_rope(q[:, sl], ones2, gq, cos, sin, bit16).astype(BF16)
    kv = _dot(h, w_ref[:, kb0:kb0 + 2 * GQA_KV_WIDTH])
    kb_ref[0] = _norm_rope(kv[:, :GQA_KV_WIDTH], ones2, gk, cos, sin, bit16).astype(BF16)
    vb_ref[0] = kv[:, GQA_KV_WIDTH:].astype(BF16)
    qa_ref[0] = (_dot(h, w_ref[:, 0:NA_WIDTH]) * Q_SCALE).astype(BF16)
    ka_ref[0] = _dot(h, w_ref[:, NA_WIDTH:2 * NA_WIDTH]).astype(BF16)
    va_ref[0] = _dot(h, w_ref[:, 2 * NA_WIDTH:3 * NA_WIDTH]).astype(BF16)


def _in_projection(x, mod, norm_g, w_in_b, ones2, gq, gk, cos_t, sin_t):
    blk_in = ROW_TILE * D_MODEL * 4
    blk_out = ROW_TILE * IN_WIDTH * 2
    row = lambda g, i: (g, i, 0)
    tab = lambda g, i: (jnp.where(g == CTX_GROUP, 1, 0), i, 0)
    const2 = lambda g, i: (0, 0)
    widths = (NA_WIDTH, NA_WIDTH, NA_WIDTH, GQA_WIDTH, GQA_KV_WIDTH, GQA_KV_WIDTH)
    return pl.pallas_call(
        _proj_kernel,
        grid=(N_GROUPS, SEQ // ROW_TILE),
        in_specs=[
            pl.BlockSpec((1, ROW_TILE, D_MODEL), row),
            pl.BlockSpec((1, 1, 3 * D_MODEL), lambda g, i: (g, 0, 1)),
            pl.BlockSpec((1, D_MODEL), const2),
            pl.BlockSpec((D_MODEL, IN_WIDTH), const2, pipeline_mode=pl.Buffered(1)),
            pl.BlockSpec((2 * V7X_LANES, V7X_LANES), const2),
            pl.BlockSpec((1, V7X_LANES), const2),
            pl.BlockSpec((1, V7X_LANES), const2),
            pl.BlockSpec((1, ROW_TILE, V7X_LANES), tab),
            pl.BlockSpec((1, ROW_TILE, V7X_LANES), tab),
        ],
        out_specs=[pl.BlockSpec((1, ROW_TILE, w), row) for w in widths],
        out_shape=[jax.ShapeDtypeStruct((N_GROUPS, SEQ, w), BF16) for w in widths],
        compiler_params=pltpu.CompilerParams(
            dimension_semantics=("parallel", "parallel"),
            vmem_limit_bytes=_vmem_limit(blk_in + blk_out, D_MODEL * IN_WIDTH * 2, 3 * blk_in)),
        name="in_projection",
    )(x, mod, norm_g.reshape(1, D_MODEL), w_in_b, ones2, gq, gk, cos_t, sin_t)


def _with_ones(v, hh):
    low = _low_lanes(v.shape[0])
    one = jnp.ones_like(v)
    return jnp.where(low, v, one) if hh == 0 else jnp.where(low, one, v)


def _normalise(o):
    return o / pltpu.roll(o, HEAD_DIM, 1)


def _softmax_pv(s, v, hh):
    e = jnp.exp2(s - s.max(axis=-1, keepdims=True)).astype(BF16)
    return _normalise(_dot(e, _with_ones(v, hh)))


def _low_lanes(n):
    return lax.broadcasted_iota(jnp.int32, (n, V7X_LANES), 1) < HEAD_DIM


def _mask_head(q, hh):
    low = _low_lanes(q.shape[0])
    zero = jnp.zeros_like(q)
    return jnp.where(low, q, zero) if hh == 0 else jnp.where(low, zero, q)


def _first_middle_last(j, last, step):
    @pl.when(j == 0)
    def _():
        step(True, False)

    @pl.when((j > 0) & (j < last))
    def _():
        step(True, True)

    @pl.when(j == last)
    def _():
        step(False, True)


def _na_first_key(i):
    first_key_row = jnp.clip(NA_Q_ROWS * i - NA_WIN_H // 2, 0, GRID_H - NA_K_ROWS)
    return pl.multiple_of(first_key_row * GRID_W, GRID_W)


def _na_kernel(q_ref, k_ref, kc_ref, bias0_ref, bias1_ref, v_ref, vc_ref, o_ref, e_ref, ec_ref):
    j = pl.program_id(0)
    bias_refs = (bias0_ref, bias1_ref)
    chains = [(t, h) for t in range(NA_STEP_TILES) for h in range(NA_HEADS)]

    def first_keys(step_index):
        return [_na_first_key((step_index * NA_STEP_TILES + t) % NA_BLOCKS) for t in range(NA_STEP_TILES)]

    def store_exp(i, s_loc, s_ctx, m):
        e_ref[i] = jnp.exp2(s_loc - m).astype(BF16)
        ec_ref[i] = jnp.exp2(s_ctx - m).astype(BF16)

    def step(scores_of_step_j, values_of_step_before):
        low = _low_lanes(NA_Q_TILE)
        k0_a = first_keys(jnp.minimum(j, NA_STEPS - 1))
        k0_b = first_keys(jnp.maximum(j - 1, 0))
        pending = None
        o_pair = []
        for i, (t, h) in enumerate(chains):
            c, hh = divmod(h, 2)
            sl = slice(c * V7X_LANES, (c + 1) * V7X_LANES)
            rows = slice(t * NA_Q_TILE, (t + 1) * NA_Q_TILE)
            if scores_of_step_j:
                qm = _mask_head(q_ref[0, rows, sl], hh)
                s_loc = _dot_nt(qm, k_ref[0, pl.ds(k0_a[t], NA_K_TILE), sl]) + bias_refs[t][0, 0, h]
                s_ctx = _dot_nt(qm, kc_ref[0, :, sl])
                m = jnp.maximum(s_loc.max(axis=-1, keepdims=True), s_ctx.max(axis=-1, keepdims=True))
                if pending is not None:
                    store_exp(*pending)
                pending = (i, s_loc, s_ctx, m)
            if values_of_step_before:
                vw = _with_ones(v_ref[0, pl.ds(k0_b[t], NA_K_TILE), sl], hh)
                vc = _with_ones(vc_ref[0, :, sl], hh)
                o_pair.append(_normalise(_dot(e_ref[i], vw) + _dot(ec_ref[i], vc)))
                if hh == 1:
                    o_ref[0, rows, sl] = jnp.where(low, o_pair[0], o_pair[1])
                    o_pair = []
        if pending is not None:
            store_exp(*pending)

    _first_middle_last(j, NA_STEPS, step)


def _na_block_picks():
    picks = np.full((3, NA_Q_ROWS, NA_K_ROWS), RPB_H, np.int32)
    for v, i in enumerate((0, 1, NA_BLOCKS - 1)):
        first_key_row = min(max(NA_Q_ROWS * i - NA_WIN_H // 2, 0), GRID_H - NA_K_ROWS)
        for a in range(NA_Q_ROWS):
            qr = NA_Q_ROWS * i + a
            rs = min(max(qr - NA_WIN_H // 2, 0), GRID_H - NA_WIN_H)
            for n in range(NA_K_ROWS):
                kr = first_key_row + n
                if rs <= kr < rs + NA_WIN_H:
                    picks[v, a, n] = kr - qr + NA_WIN_H - 1
    return picks


def _na_bias_kernel(ring_ref, o_ref, *, picks):
    ring = ring_ref[0, 0]
    qc = lax.broadcasted_iota(jnp.int32, (GRID_W, V7X_LANES), 0)
    kc = lax.broadcasted_iota(jnp.int32, (GRID_W, V7X_LANES), 1)
    cs = jnp.clip(qc - NA_WIN_W // 2, 0, GRID_W - NA_WIN_W)
    in_win = (kc >= cs) & (kc < cs + NA_WIN_W)
    blocks = []
    for r in range(RPB_H):
        rows = jnp.broadcast_to(ring[r:r + 1, :], (GRID_W, V7X_LANES))
        toe = pltpu.roll(rows, 0, 1, stride=1, stride_axis=0)
        blocks.append(jnp.where(in_win, toe * LOG2_E, NEG_INF)[:, :GRID_W])
    blocks.append(jnp.full((GRID_W, GRID_W), NEG_INF, F32))
    for v in range(picks.shape[0]):
        for a in range(NA_Q_ROWS):
            o_ref[0, v, 0, a * GRID_W:(a + 1) * GRID_W, :] = jnp.concatenate(
                [blocks[p] for p in picks[v, a]], axis=1)


def _na_bias_table(rpb):
    layers = rpb.shape[0]
    pad_rows = 16 - RPB_H
    ring = jnp.concatenate(
        [rpb[..., NA_WIN_W - 1:], jnp.zeros(rpb.shape[:3] + (V7X_LANES - RPB_W,), F32),
         rpb[..., :NA_WIN_W - 1]], axis=-1)
    ring = jnp.pad(ring, ((0, 0), (0, 0), (0, pad_rows), (0, 0)))
    return pl.pallas_call(
        functools.partial(_na_bias_kernel, picks=_na_block_picks()),
        grid=(layers, NA_HEADS),
        in_specs=[pl.BlockSpec((1, 1, RPB_H + pad_rows, V7X_LANES), lambda l, h: (l, h, 0, 0))],
        out_specs=pl.BlockSpec((1, 3, 1, NA_Q_TILE, NA_K_TILE), lambda l, h: (l, 0, h, 0, 0)),
        out_shape=jax.ShapeDtypeStruct((layers, 3, NA_HEADS, NA_Q_TILE, NA_K_TILE), F32),
        compiler_params=pltpu.CompilerParams(dimension_semantics=("parallel", "parallel")),
        name="na_bias_table",
    )(ring)


def _na_attention(qa, ka, va, bias, layer):
    step_a = lambda j: jnp.minimum(j, NA_STEPS - 1)
    step_b = lambda j: jnp.maximum(j - 1, 0)
    per_batch = NA_BLOCKS // NA_STEP_TILES
    chains = NA_STEP_TILES * NA_HEADS
    rows = NA_STEP_TILES * NA_Q_TILE

    def variant(t):
        def index(j):
            i = (step_a(j) * NA_STEP_TILES + t) % NA_BLOCKS
            return (layer, jnp.where(i == 0, 0, jnp.where(i == NA_BLOCKS - 1, 2, 1)), 0, 0, 0)
        return index

    blk = (rows * NA_WIDTH * (2 + 4) + 2 * (SEQ + CTX_LEN) * NA_WIDTH * 2
           + chains * NA_Q_TILE * NA_K_TILE * 4)
    scratch = chains * NA_Q_TILE * (NA_K_TILE + CTX_LEN) * 2
    scores = NA_HEADS * NA_Q_TILE * (NA_K_TILE + CTX_LEN) * 4
    bias_spec = lambda t: pl.BlockSpec((1, 1, NA_HEADS, NA_Q_TILE, NA_K_TILE), variant(t))
    return pl.pallas_call(
        _na_kernel,
        grid=(NA_STEPS + 1,),
        in_specs=[
            pl.BlockSpec((1, rows, NA_WIDTH), lambda j: (step_a(j) // per_batch, step_a(j) % per_batch, 0)),
            pl.BlockSpec((1, SEQ, NA_WIDTH), lambda j: (step_a(j) // per_batch, 0, 0)),
            pl.BlockSpec((1, CTX_LEN, NA_WIDTH), lambda j: (CTX_GROUP, step_a(j) // per_batch, 0)),
            bias_spec(0), bias_spec(1),
            pl.BlockSpec((1, SEQ, NA_WIDTH), lambda j: (step_b(j) // per_batch, 0, 0)),
            pl.BlockSpec((1, CTX_LEN, NA_WIDTH), lambda j: (CTX_GROUP, step_b(j) // per_batch, 0)),
        ],
        out_specs=pl.BlockSpec((1, rows, NA_WIDTH),
                               lambda j: (step_b(j) // per_batch, step_b(j) % per_batch, 0)),
        out_shape=jax.ShapeDtypeStruct((BATCH, SEQ, NA_WIDTH), F32),
        scratch_shapes=[pltpu.VMEM((chains, NA_Q_TILE, NA_K_TILE), BF16),
                        pltpu.VMEM((chains, NA_Q_TILE, CTX_LEN), BF16)],
        compiler_params=pltpu.CompilerParams(
            dimension_semantics=("arbitrary",),
            vmem_limit_bytes=_vmem_limit(blk, scratch, 2 * scores)),
        name="neighbourhood_attention",
    )(qa, ka, ka, bias, bias, va, va)


def _gqa_stack(q, kvh):
    return jnp.concatenate(
        [_mask_head(q[:, c * V7X_LANES:(c + 1) * V7X_LANES], kvh) for c in range(GQA_GROUP)], axis=0)


def _gqa_unstack(o_kv0, o_kv1, n_q):
    low = _low_lanes(n_q)
    return [jnp.where(low, o_kv0[c * n_q:(c + 1) * n_q], o_kv1[c * n_q:(c + 1) * n_q])
            for c in range(GQA_GROUP)]


def _gqa_kernel(q_ref, kt_ref, kct_ref, v_ref, vc_ref, o_ref, e_ref):
    j = pl.program_id(0)
    chains = [(t, kvh) for t in range(GQA_Q_TILE // GQA_SUB_TILE) for kvh in range(GQA_KV_HEADS)]
    n_lat = SEQ // GQA_KEY_CHUNK
    n_chunks = n_lat + CTX_LEN // GQA_KEY_CHUNK

    def chunk(ref_lat, ref_ctx, c, axis):
        src, cc = (ref_lat, c) if c < n_lat else (ref_ctx, c - n_lat)
        keys = slice(cc * GQA_KEY_CHUNK, (cc + 1) * GQA_KEY_CHUNK)
        return src[keys] if axis == 0 else src[0, :, keys]

    def step(scores_of_tile_j, values_of_tile_before):
        if values_of_tile_before:
            v_ones = [(_with_ones(v_ref[0], kvh), _with_ones(vc_ref[0], kvh)) for kvh in range(GQA_KV_HEADS)]
        scores, row_max, outs = {}, {}, {}
        for i in range(len(chains) + scores_of_tile_j):
            live = i < len(chains)
            if live:
                t, kvh = chains[i]
                rows = slice(t * GQA_SUB_TILE, (t + 1) * GQA_SUB_TILE)
            if live and scores_of_tile_j:
                stack = _gqa_stack(q_ref[0, rows, :], kvh)
                scores[i], lane_max = [], None
            acc = None
            for c in range(n_chunks):
                keys = slice(c * GQA_KEY_CHUNK, (c + 1) * GQA_KEY_CHUNK)
                if live and scores_of_tile_j:
                    s = _dot(stack, chunk(kt_ref, kct_ref, c, 1))
                    scores[i].append(s)
                    for w in range(GQA_KEY_CHUNK // V7X_LANES):
                        part = s[:, w * V7X_LANES:(w + 1) * V7X_LANES]
                        lane_max = part if lane_max is None else jnp.maximum(lane_max, part)
                if i >= 1 and scores_of_tile_j:
                    e_ref[i - 1, :, keys] = jnp.exp2(scores[i - 1][c] - row_max[i - 1]).astype(BF16)
                if live and values_of_tile_before:
                    part = _dot(e_ref[i, :, keys], chunk(v_ones[kvh][0], v_ones[kvh][1], c, 0))
                    acc = part if acc is None else acc + part
            if i >= 1 and scores_of_tile_j:
                del scores[i - 1]
            if live and scores_of_tile_j:
                row_max[i] = lane_max.max(axis=-1, keepdims=True)
            if live and values_of_tile_before:
                outs[kvh] = _normalise(acc)
                if kvh == GQA_KV_HEADS - 1:
                    for col, oc in enumerate(_gqa_unstack(outs[0], outs[1], GQA_SUB_TILE)):
                        o_ref[0, rows, col * V7X_LANES:(col + 1) * V7X_LANES] = oc

    _first_middle_last(j, GQA_TILES, step)


def _gqa_attention(qb, kbt, vb):
    tile_a = lambda j: jnp.minimum(j, GQA_TILES - 1)
    tile_b = lambda j: jnp.maximum(j - 1, 0)
    per_b = SEQ // GQA_Q_TILE
    chains = GQA_KV_HEADS * GQA_Q_TILE // GQA_SUB_TILE
    chain_rows = GQA_GROUP * GQA_SUB_TILE
    blk = (GQA_Q_TILE * GQA_WIDTH * (2 + 4) + 2 * (SEQ + CTX_LEN) * V7X_LANES * 2)
    scratch = chains * chain_rows * (SEQ + CTX_LEN) * 2
    scores = chain_rows * (SEQ + CTX_LEN) * 4
    return pl.pallas_call(
        _gqa_kernel,
        grid=(GQA_TILES + 1,),
        in_specs=[
            pl.BlockSpec((1, GQA_Q_TILE, GQA_WIDTH), lambda j: (tile_a(j) // per_b, tile_a(j) % per_b, 0)),
            pl.BlockSpec((1, GQA_KV_WIDTH, SEQ), lambda j: (tile_a(j) // per_b, 0, 0)),
            pl.BlockSpec((1, GQA_KV_WIDTH, CTX_LEN), lambda j: (CTX_GROUP, 0, tile_a(j) // per_b)),
            pl.BlockSpec((1, SEQ, GQA_KV_WIDTH), lambda j: (tile_b(j) // per_b, 0, 0)),
            pl.BlockSpec((1, CTX_LEN, GQA_KV_WIDTH), lambda j: (CTX_GROUP, tile_b(j) // per_b, 0)),
        ],
        out_specs=pl.BlockSpec((1, GQA_Q_TILE, GQA_WIDTH), lambda j: (tile_b(j) // per_b, tile_b(j) % per_b, 0)),
        out_shape=jax.ShapeDtypeStruct((BATCH, SEQ, GQA_WIDTH), F32),
        scratch_shapes=[pltpu.VMEM((chains, chain_rows, SEQ + CTX_LEN), BF16)],
        compiler_params=pltpu.CompilerParams(
            dimension_semantics=("arbitrary",),
            vmem_limit_bytes=_vmem_limit(blk, scratch, 4 * scores)),
        name="gqa_attention",
    )(qb, kbt, kbt, vb, vb)


def _ctx_kernel(qa_ref, ka_ref, va_ref, qb_ref, kbt_ref, vb_ref, oa_ref, ob_ref):
    low = _low_lanes(CTX_LEN)
    for b in range(CTX_STEP_BATCHES):
        rows = slice(b * CTX_LEN, (b + 1) * CTX_LEN)
        for c in range(NA_WIDTH // V7X_LANES):
            sl = slice(c * V7X_LANES, (c + 1) * V7X_LANES)
            q, k, v = qa_ref[0, rows, sl], ka_ref[0, rows, sl], va_ref[0, rows, sl]
            o = [_softmax_pv(_dot_nt(_mask_head(q, hh), k), v, hh) for hh in range(2)]
            oa_ref[0, rows, sl] = jnp.where(low, o[0], o[1])
        q = qb_ref[0, rows, :]
        o = [_softmax_pv(_dot(_gqa_stack(q, kvh), kbt_ref[0, :, rows]), vb_ref[0, rows, :], kvh)
             for kvh in range(GQA_KV_HEADS)]
        for c, oc in enumerate(_gqa_unstack(o[0], o[1], CTX_LEN)):
            ob_ref[0, rows, c * V7X_LANES:(c + 1) * V7X_LANES] = oc


def _ctx_attention(qa, ka, va, qb, kbt, vb):
    n_rows = CTX_STEP_BATCHES * CTX_LEN
    row = lambda b: (CTX_GROUP, b, 0)
    wide = pl.BlockSpec((1, n_rows, NA_WIDTH), row)
    narrow = pl.BlockSpec((1, n_rows, GQA_KV_WIDTH), row)
    narrow_t = pl.BlockSpec((1, GQA_KV_WIDTH, n_rows), lambda b: (CTX_GROUP, 0, b))
    out = pl.BlockSpec((1, n_rows, NA_WIDTH), lambda b: (0, b, 0))
    shape = jax.ShapeDtypeStruct((1, SEQ, NA_WIDTH), F32)
    blk = n_rows * (4 * NA_WIDTH * 2 + 2 * GQA_KV_WIDTH * 2 + 2 * NA_WIDTH * 4)
    return pl.pallas_call(
        _ctx_kernel,
        grid=(BATCH // CTX_STEP_BATCHES,),
        in_specs=[wide, wide, wide, wide, narrow_t, narrow],
        out_specs=[out, out],
        out_shape=[shape, shape],
        compiler_params=pltpu.CompilerParams(
            dimension_semantics=("parallel",),
            vmem_limit_bytes=_vmem_limit(blk, 0, 16 * GQA_GROUP * CTX_LEN * CTX_LEN * 4)),
        name="context_attention",
    )(qa, ka, va, qb, kbt, vb)


def _outproj_kernel(*refs, split_ctx):
    if split_ctx:
        x_ref, oa_ref, ob_ref, oac_ref, obc_ref, mod_ref, ga_ref, gb_ref, w_ref, o_ref = refs
        is_ctx = pl.program_id(0) == CTX_GROUP
        oa = jnp.where(is_ctx, oac_ref[0], oa_ref[0])
        ob = jnp.where(is_ctx, obc_ref[0], ob_ref[0])
    else:
        x_ref, oa_ref, ob_ref, mod_ref, ga_ref, gb_ref, w_ref, o_ref = refs
        oa, ob = oa_ref[0], ob_ref[0]
    gate = mod_ref[0][:, 2 * D_MODEL:]
    ya = (_rms(oa) * ga_ref[...]).astype(BF16)
    yb = (_rms(ob) * gb_ref[...]).astype(BF16)
    y = _dot(ya, w_ref[:NA_WIDTH, :]) + _dot(yb, w_ref[NA_WIDTH:, :])
    o_ref[0] = x_ref[0] + gate * y


def _out_projection(x, oa, ob, ctx_o, mod, ga, gb, w_out_b):
    split_ctx = ctx_o is not None
    n_groups = N_GROUPS if split_ctx else BATCH
    row = lambda g, i: (g, i, 0)
    const2 = lambda g, i: (0, 0)
    last_tile = SEQ // ROW_TILE - 1
    lat = lambda g, i: (jnp.minimum(g, BATCH - 1), jnp.where(g == CTX_GROUP, last_tile, i), 0)
    cxt = lambda g, i: (0, jnp.where(g == CTX_GROUP, i, 0), 0)
    att = lambda imap: pl.BlockSpec((1, ROW_TILE, NA_WIDTH), imap)
    att_specs = [att(lat), att(lat), att(cxt), att(cxt)] if split_ctx else [att(row), att(row)]
    att_args = (oa, ob) + (tuple(ctx_o) if split_ctx else ())
    blk = ROW_TILE * (2 * D_MODEL + (2 + 2 * split_ctx) * NA_WIDTH) * 4
    return pl.pallas_call(
        functools.partial(_outproj_kernel, split_ctx=split_ctx),
        grid=(n_groups, SEQ // ROW_TILE),
        in_specs=[pl.BlockSpec((1, ROW_TILE, D_MODEL), row)] + att_specs + [
            pl.BlockSpec((1, 1, 3 * D_MODEL), lambda g, i: (g, 0, 1)),
            pl.BlockSpec((1, NA_WIDTH), const2),
            pl.BlockSpec((1, GQA_WIDTH), const2),
            pl.BlockSpec((NA_WIDTH + GQA_WIDTH, D_MODEL), const2, pipeline_mode=pl.Buffered(1)),
        ],
        out_specs=pl.BlockSpec((1, ROW_TILE, D_MODEL), row),
        out_shape=jax.ShapeDtypeStruct((n_groups, SEQ, D_MODEL), F32),
        compiler_params=pltpu.CompilerParams(
            dimension_semantics=("arbitrary", "arbitrary"),
            vmem_limit_bytes=_vmem_limit(blk, (NA_WIDTH + GQA_WIDTH) * D_MODEL * 2, blk)),
        name="out_projection_ctx" if split_ctx else "out_projection",
    )(x, *att_args, mod, ga.reshape(1, NA_WIDTH), gb.reshape(1, GQA_WIDTH), w_out_b)


def _gqa_head_permutation():
    p = np.arange(GQA_WIDTH)
    col, half, d = p // V7X_LANES, (p % V7X_LANES) // HEAD_DIM, p % HEAD_DIM
    return HEAD_DIM * (col + GQA_GROUP * half) + d


def _rope_tables():
    t = jnp.arange(SEQ, dtype=jnp.int32)
    row = (t // GRID_W).astype(F32)
    col = (t % GRID_W).astype(F32)
    axis_dim = HEAD_DIM // 2
    inv_freq = ROPE_THETA ** (-jnp.arange(0, axis_dim, 2, dtype=F32) / axis_dim)
    ang_r = row[:, None] * inv_freq[None, :]
    ang_c = col[:, None] * inv_freq[None, :]
    cos_r, sin_r, cos_c, sin_c = jnp.cos(ang_r), jnp.sin(ang_r), jnp.cos(ang_c), jnp.sin(ang_c)
    cos_h = jnp.concatenate([cos_r, cos_r, cos_c, cos_c], axis=-1)
    sin_h = jnp.concatenate([-sin_r, sin_r, -sin_c, sin_c], axis=-1)
    heads = V7X_LANES // HEAD_DIM
    cos2 = jnp.tile(cos_h, (1, heads))
    sin2 = jnp.tile(sin_h, (1, heads))
    return (jnp.stack([cos2, jnp.ones_like(cos2)]), jnp.stack([sin2, jnp.zeros_like(sin2)]))


def kernel(x, c, ctx, c_ctx, w_mod, b_mod, norm_ffn1, ffn1_w13, ffn1_w2, norm_mix, w_in, na_rpb,
           gqa_q_norm, gqa_k_norm, out_norm_a, out_norm_b, w_out, norm_ffn2, ffn2_w13, ffn2_w2, norm_f):
    perm = _gqa_head_permutation()
    cos_t, sin_t = _rope_tables()
    head_of_lane = np.arange(V7X_LANES) // HEAD_DIM
    ones_blk = np.asarray(head_of_lane[:, None] == head_of_lane[None, :], np.float32)
    ones2 = jnp.asarray(np.concatenate([ones_blk, ones_blk], axis=0), BF16)
    heads_per_col = V7X_LANES // HEAD_DIM
    bias_all = _na_bias_table(na_rpb)

    cond = jnp.zeros((COND_ROWS, D_MODEL), F32).at[:BATCH].set(c).at[CTX_GROUP].set(c_ctx)
    mod_all = _modulation(cond, w_mod, b_mod)

    w13b_1, w2b_1 = _cast_bf16(ffn1_w13), _cast_bf16(ffn1_w2)
    w13b_2, w2b_2 = _cast_bf16(ffn2_w13), _cast_bf16(ffn2_w2)

    xs = x
    for l in range(DEPTH):
        last = l == DEPTH - 1
        mod = mod_all[l].reshape(COND_ROWS, 1, N_MOD * D_MODEL)
        qb0 = 3 * NA_WIDTH
        w_in_l = w_in[l]
        w_in_b = jnp.concatenate(
            [w_in_l[:, :qb0], w_in_l[:, qb0:qb0 + GQA_WIDTH][:, perm], w_in_l[:, qb0 + GQA_WIDTH:]],
            axis=1).astype(BF16)
        w_out_l = w_out[l]
        w_out_b = jnp.concatenate([w_out_l[:NA_WIDTH], w_out_l[NA_WIDTH:][perm]], axis=0).astype(BF16)
        gq = jnp.tile(gqa_q_norm[l], heads_per_col).reshape(1, V7X_LANES)
        gk = jnp.tile(gqa_k_norm[l], heads_per_col).reshape(1, V7X_LANES)

        xs = _ffn_half(xs, mod, 0, norm_ffn1[l], w13b_1, w2b_1, l, N_GROUPS,
                       ctx=ctx.reshape(1, SEQ, D_MODEL) if l == 0 else None)
        qa, ka, va, qb, kb, vb = _in_projection(xs, mod, norm_mix[l], w_in_b, ones2, gq, gk,
                                                cos_t, sin_t)
        kbt = jnp.swapaxes(kb, 1, 2)
        oa = _na_attention(qa, ka, va, bias_all, l)
        ob = _gqa_attention(qb, kbt, vb)
        ctx_o = None if last else _ctx_attention(qa, ka, va, qb, kbt, vb)
        xs = _out_projection(xs, oa, ob, ctx_o, mod, out_norm_a[l], out_norm_b[l][perm], w_out_b)
        xs = _ffn_half(xs, mod, 2, norm_ffn2[l], w13b_2, w2b_2, l, BATCH if last else N_GROUPS,
                       final_g=norm_f if last else None)
    return xs
```

```python
import functools

import numpy as np
import jax
import jax.numpy as jnp
from jax import lax
from jax.experimental import pallas as pl
from jax.experimental.pallas import tpu as pltpu

F32 = jnp.float32
BF16 = jnp.bfloat16

D_MODEL = 1024
BATCH = 8
SEQ = 2048
DEPTH = 2
GRID_W = 64
GRID_H = SEQ // GRID_W
CTX_LEN = 256
HEAD_DIM = 64
NA_HEADS = 8
GQA_HEADS = 8
GQA_KV_HEADS = 2
GQA_GROUP = GQA_HEADS // GQA_KV_HEADS
NA_WIDTH = NA_HEADS * HEAD_DIM
GQA_WIDTH = GQA_HEADS * HEAD_DIM
GQA_KV_WIDTH = GQA_KV_HEADS * HEAD_DIM
IN_WIDTH = 3 * NA_WIDTH + GQA_WIDTH + 2 * GQA_KV_WIDTH
NA_WIN_H = 8
NA_WIN_W = 16
RPB_H = 2 * NA_WIN_H - 1
RPB_W = 2 * NA_WIN_W - 1
D_FF = 2816
N_MOD = 9
ROPE_THETA = 10000.0
EPS = 1e-6
NEG_INF = -1e30
LOG2_E = 1.4426950408889634
Q_SCALE = LOG2_E * HEAD_DIM ** -0.5

N_GROUPS = BATCH + 1
CTX_GROUP = BATCH
COND_ROWS = 16

V7X_LANES = 128
V7X_VMEM_BYTES = 64 * 1024 * 1024
V7X_VMEM_REQUEST_CAP = 56 * 1024 * 1024

ROW_TILE = 1024
SPLIT_ROW_TILE = 512
FF_CHUNK = 256
N_FF_CHUNKS = D_FF // FF_CHUNK
MOD_TILE = 1152
NA_Q_ROWS = 4
NA_Q_TILE = NA_Q_ROWS * GRID_W
NA_K_ROWS = NA_Q_ROWS + NA_WIN_H - 1
NA_K_TILE = NA_K_ROWS * GRID_W
NA_BLOCKS = GRID_H // NA_Q_ROWS
CTX_STEP_BATCHES = 2
NA_STEP_TILES = 2
NA_STEPS = BATCH * NA_BLOCKS // NA_STEP_TILES
GQA_Q_TILE = 512
GQA_SUB_TILE = 64
GQA_TILES = BATCH * (SEQ // GQA_Q_TILE)
GQA_KEY_CHUNK = 256

assert BATCH * CTX_LEN == SEQ
assert D_FF % FF_CHUNK == 0 and SEQ % ROW_TILE == 0 and (N_MOD * D_MODEL) % MOD_TILE == 0
assert SEQ % SPLIT_ROW_TILE == 0
assert GRID_H % NA_Q_ROWS == 0 and SEQ % GQA_Q_TILE == 0 and GQA_Q_TILE % GQA_SUB_TILE == 0
assert SEQ % GQA_KEY_CHUNK == 0 and CTX_LEN % GQA_KEY_CHUNK == 0
assert NA_BLOCKS % NA_STEP_TILES == 0 and NA_STEP_TILES == 2


def _vmem_limit(block_bytes, scratch_bytes, temp_bytes):
    need = 2 * block_bytes + scratch_bytes + temp_bytes
    return int(min(max(need, 16 * 1024 * 1024), V7X_VMEM_REQUEST_CAP))


def _rms(x):
    return x * lax.rsqrt(jnp.mean(x * x, axis=-1, keepdims=True) + EPS)


def _dot(a, b):
    return jnp.dot(a, b, preferred_element_type=F32)


def _dot_nt(a, b):
    return lax.dot_general(a, b, (((1,), (1,)), ((), ())), preferred_element_type=F32)


def _mod_kernel(c_ref, w_ref, b_ref, o_ref):
    c = c_ref[...]
    s = (c * jax.nn.sigmoid(c)).astype(BF16)
    o_ref[0] = _dot(s, w_ref[0].astype(BF16)) + b_ref[0]


def _modulation(cond, w_mod, b_mod):
    n = N_MOD * D_MODEL
    return pl.pallas_call(
        _mod_kernel,
        grid=(DEPTH, n // MOD_TILE),
        in_specs=[
            pl.BlockSpec((COND_ROWS, D_MODEL), lambda l, j: (0, 0)),
            pl.BlockSpec((1, D_MODEL, MOD_TILE), lambda l, j: (l, 0, j)),
            pl.BlockSpec((1, 1, MOD_TILE), lambda l, j: (l, 0, j)),
        ],
        out_specs=pl.BlockSpec((1, COND_ROWS, MOD_TILE), lambda l, j: (l, 0, j)),
        out_shape=jax.ShapeDtypeStruct((DEPTH, COND_ROWS, n), F32),
        compiler_params=pltpu.CompilerParams(
            dimension_semantics=("parallel", "parallel"),
            vmem_limit_bytes=_vmem_limit(D_MODEL * MOD_TILE * 4, 0, D_MODEL * MOD_TILE * 2)),
        name="adaln_modulation",
    )(cond, w_mod, b_mod.reshape(DEPTH, 1, n))


CAST_BLOCK_BYTES = 6 * 1024 * 1024


def _cast_kernel(w_ref, o_ref):
    o_ref[...] = w_ref[...].astype(BF16)


def _cast_bf16(w):
    layers, rows, cols = w.shape
    tile = rows
    while tile * cols * 4 > CAST_BLOCK_BYTES and tile % 32 == 0:
        tile //= 2
    spec = pl.BlockSpec((1, tile, cols), lambda l, i: (l, i, 0))
    return pl.pallas_call(
        _cast_kernel,
        grid=(layers, rows // tile),
        in_specs=[spec],
        out_specs=spec,
        out_shape=jax.ShapeDtypeStruct(w.shape, BF16),
        compiler_params=pltpu.CompilerParams(
            dimension_semantics=("parallel", "parallel"),
            vmem_limit_bytes=_vmem_limit(tile * cols * 6, 0, tile * cols * 2)),
        name="cast_bf16",
    )(w)


def _ffn_kernel(*refs, final, split_ctx):
    if split_ctx:
        x_ref, ctx_ref, mod_ref, g_ref, w13_ref, w2_ref, gf_ref, o_ref, u_ref = refs
        x = jnp.where(pl.program_id(0) == CTX_GROUP, ctx_ref[0], x_ref[0])
    else:
        x_ref, mod_ref, g_ref, w13_ref, w2_ref, gf_ref, o_ref, u_ref = refs
        x = x_ref[0]
    m = mod_ref[0]
    shift, scale, gate = m[:, :D_MODEL], m[:, D_MODEL:2 * D_MODEL], m[:, 2 * D_MODEL:]
    h = ((_rms(x) * g_ref[...]) * (1.0 + scale) + shift).astype(BF16)
    for j in range(N_FF_CHUNKS):
        a = _dot(h, w13_ref[0, :, j * FF_CHUNK:(j + 1) * FF_CHUNK])
        g = _dot(h, w13_ref[0, :, D_FF + j * FF_CHUNK:D_FF + (j + 1) * FF_CHUNK])
        u_ref[:, j * FF_CHUNK:(j + 1) * FF_CHUNK] = ((g * jax.nn.sigmoid(g)) * a).astype(BF16)
    y = _dot(u_ref[...], w2_ref[0])
    out = x + (0.5 * gate) * y
    if final:
        out = _rms(out) * gf_ref[...]
    o_ref[0] = out


def _ffn_half(x, mod, sub, norm_g, w13b, w2b, layer, n_groups, ctx=None, final_g=None):
    final = final_g is not None
    split_ctx = ctx is not None
    gf = final_g if final else norm_g
    tile = SPLIT_ROW_TILE if split_ctx else ROW_TILE
    blk = tile * D_MODEL * 4
    wbytes = 3 * D_FF * D_MODEL * 2
    last_tile = SEQ // tile - 1
    if split_ctx:
        x_specs = [
            pl.BlockSpec((1, tile, D_MODEL),
                         lambda g, i: (jnp.minimum(g, BATCH - 1), jnp.where(g == CTX_GROUP, last_tile, i), 0)),
            pl.BlockSpec((1, tile, D_MODEL), lambda g, i: (0, jnp.where(g == CTX_GROUP, i, 0), 0)),
        ]
        xs = (x, ctx)
    else:
        x_specs = [pl.BlockSpec((1, tile, D_MODEL), lambda g, i: (g, i, 0))]
        xs = (x,)
    return pl.pallas_call(
        functools.partial(_ffn_kernel, final=final, split_ctx=split_ctx),
        grid=(n_groups, SEQ // tile),
        in_specs=x_specs + [
            pl.BlockSpec((1, 1, 3 * D_MODEL), lambda g, i: (g, 0, sub)),
            pl.BlockSpec((1, D_MODEL), lambda g, i: (0, 0)),
            pl.BlockSpec((1, D_MODEL, 2 * D_FF), lambda g, i: (layer, 0, 0), pipeline_mode=pl.Buffered(1)),
            pl.BlockSpec((1, D_FF, D_MODEL), lambda g, i: (layer, 0, 0), pipeline_mode=pl.Buffered(1)),
            pl.BlockSpec((1, D_MODEL), lambda g, i: (0, 0)),
        ],
        out_specs=pl.BlockSpec((1, tile, D_MODEL), lambda g, i: (g, i, 0)),
        out_shape=jax.ShapeDtypeStruct((n_groups, SEQ, D_MODEL), F32),
        scratch_shapes=[pltpu.VMEM((tile, D_FF), BF16)],
        compiler_params=pltpu.CompilerParams(
            dimension_semantics=("arbitrary", "arbitrary"),
            vmem_limit_bytes=_vmem_limit((2 + split_ctx) * blk, wbytes + tile * D_FF * 2, 3 * blk)),
        name="ffn_half_final" if final else ("ffn_half_first" if split_ctx else "ffn_half"),
    )(*xs, mod, norm_g.reshape(1, D_MODEL), w13b, w2b, gf.reshape(1, D_MODEL))


def _norm_rope(x, ones2, gain, cos, sin, bit16):
    sq = x * x
    hi = sq.astype(BF16)
    lo = (sq - hi.astype(F32)).astype(BF16)
    ssum = _dot(jnp.concatenate([hi, lo], axis=1), ones2)
    y = (x * lax.rsqrt(ssum * (1.0 / HEAD_DIM) + EPS)) * gain
    quarter = HEAD_DIM // 4
    partner = jnp.where(bit16, pltpu.roll(y, quarter, 1), pltpu.roll(y, V7X_LANES - quarter, 1))
    return y * cos + partner * sin


def _proj_kernel(x_ref, mod_ref, g_ref, w_ref, ones_ref, gq_ref, gk_ref, cos_ref, sin_ref,
                 qa_ref, ka_ref, va_ref, qb_ref, kbt_ref, vb_ref):
    x = x_ref[0]
    m = mod_ref[0]
    shift, scale = m[:, :D_MODEL], m[:, D_MODEL:2 * D_MODEL]
    h = ((_rms(x) * g_ref[...]) * (1.0 + scale) + shift).astype(BF16)
    qb0 = 3 * NA_WIDTH
    kb0 = qb0 + GQA_WIDTH
    cos = cos_ref[0]
    sin = sin_ref[0]
    ones2 = ones_ref[...]
    lane = lax.broadcasted_iota(jnp.int32, (ROW_TILE, V7X_LANES), 1)
    bit16 = (lane & (HEAD_DIM // 4)) != 0
    gq = gq_ref[...] * Q_SCALE
    gk = gk_ref[...]
    q = _dot(h, w_ref[0, :, qb0:kb0])
    for c in range(GQA_WIDTH // V7X_LANES):
        sl = slice(c * V7X_LANES, (c + 1) * V7X_LANES)
        qb_ref[0, :, sl] = _norm_rope(q[:, sl], ones2, gq, cos, sin, bit16).astype(BF16)
    kv = _dot(h, w_ref[0, :, kb0:kb0 + 2 * GQA_KV_WIDTH])
    kbt_ref[0] = _norm_rope(kv[:, :GQA_KV_WIDTH], ones2, gk, cos, sin, bit16).T.astype(BF16)
    vb_ref[0] = kv[:, GQA_KV_WIDTH:].astype(BF16)
    qa_ref[0] = (_dot(h, w_ref[0, :, 0:NA_WIDTH]) * Q_SCALE).astype(BF16)
    ka_ref[0] = _dot(h, w_ref[0, :, NA_WIDTH:2 * NA_WIDTH]).astype(BF16)
    va_ref[0] = _dot(h, w_ref[0, :, 2 * NA_WIDTH:3 * NA_WIDTH]).astype(BF16)


def _in_projection(x, mod, norm_g, w_in_b, layer, ones2, gq, gk, cos_t, sin_t):
    blk_in = ROW_TILE * D_MODEL * 4
    blk_out = ROW_TILE * IN_WIDTH * 2
    row = lambda g, i: (g, i, 0)
    tab = lambda g, i: (jnp.where(g == CTX_GROUP, 1, 0), i, 0)
    const2 = lambda g, i: (0, 0)
    rows_out = lambda w: (pl.BlockSpec((1, ROW_TILE, w), row), jax.ShapeDtypeStruct((N_GROUPS, SEQ, w), BF16))
    cols_out = lambda w: (pl.BlockSpec((1, w, ROW_TILE), lambda g, i: (g, 0, i)),
                          jax.ShapeDtypeStruct((N_GROUPS, w, SEQ), BF16))
    outs = [rows_out(NA_WIDTH), rows_out(NA_WIDTH), rows_out(NA_WIDTH), rows_out(GQA_WIDTH),
            cols_out(GQA_KV_WIDTH), rows_out(GQA_KV_WIDTH)]
    return pl.pallas_call(
        _proj_kernel,
        grid=(N_GROUPS, SEQ // ROW_TILE),
        in_specs=[
            pl.BlockSpec((1, ROW_TILE, D_MODEL), row),
            pl.BlockSpec((1, 1, 3 * D_MODEL), lambda g, i: (g, 0, 1)),
            pl.BlockSpec((1, D_MODEL), const2),
            pl.BlockSpec((1, D_MODEL, IN_WIDTH), lambda g, i: (layer, 0, 0), pipeline_mode=pl.Buffered(1)),
            pl.BlockSpec((2 * V7X_LANES, V7X_LANES), const2),
            pl.BlockSpec((1, V7X_LANES), const2),
            pl.BlockSpec((1, V7X_LANES), const2),
            pl.BlockSpec((1, ROW_TILE, V7X_LANES), tab),
            pl.BlockSpec((1, ROW_TILE, V7X_LANES), tab),
        ],
        out_specs=[spec for spec, _ in outs],
        out_shape=[shape for _, shape in outs],
        compiler_params=pltpu.CompilerParams(
            dimension_semantics=("parallel", "parallel"),
            vmem_limit_bytes=_vmem_limit(blk_in + blk_out, D_MODEL * IN_WIDTH * 2, 3 * blk_in)),
        name="in_projection",
    )(x, mod, norm_g.reshape(1, D_MODEL), w_in_b, ones2, gq, gk, cos_t, sin_t)


def _with_ones(v, hh):
    low = _low_lanes(v.shape[0])
    one = jnp.ones_like(v)
    return jnp.where(low, v, one) if hh == 0 else jnp.where(low, one, v)


def _normalise(o):
    return o / pltpu.roll(o, HEAD_DIM, 1)


def _softmax_pv(s, v, hh):
    e = jnp.exp2(s - s.max(axis=-1, keepdims=True)).astype(BF16)
    return _normalise(_dot(e, _with_ones(v, hh)))


def _low_lanes(n):
    return lax.broadcasted_iota(jnp.int32, (n, V7X_LANES), 1) < HEAD_DIM


def _mask_head(q, hh):
    low = _low_lanes(q.shape[0])
    zero = jnp.zeros_like(q)
    return jnp.where(low, q, zero) if hh == 0 else jnp.where(low, zero, q)


def _first_middle_last(j, last, step):
    @pl.when(j == 0)
    def _():
        step(True, False)

    @pl.when((j > 0) & (j < last))
    def _():
        step(True, True)

    @pl.when(j == last)
    def _():
        step(False, True)


def _na_first_key(i):
    first_key_row = jnp.clip(NA_Q_ROWS * i - NA_WIN_H // 2, 0, GRID_H - NA_K_ROWS)
    return pl.multiple_of(first_key_row * GRID_W, GRID_W)


def _na_kernel(q_ref, k_ref, kc_ref, bias0_ref, bias1_ref, v_ref, vc_ref, o_ref, e_ref, ec_ref):
    j = pl.program_id(0)
    bias_refs = (bias0_ref, bias1_ref)
    chains = [(t, h) for t in range(NA_STEP_TILES) for h in range(NA_HEADS)]

    def first_keys(step_index):
        return [_na_first_key((step_index * NA_STEP_TILES + t) % NA_BLOCKS) for t in range(NA_STEP_TILES)]

    def store_exp(i, s_loc, s_ctx, m):
        e_ref[i] = jnp.exp2(s_loc - m).astype(BF16)
        ec_ref[i] = jnp.exp2(s_ctx - m).astype(BF16)

    def step(scores_of_step_j, values_of_step_before):
        low = _low_lanes(NA_Q_TILE)
        k0_a = first_keys(jnp.minimum(j, NA_STEPS - 1))
        k0_b = first_keys(jnp.maximum(j - 1, 0))
        pending = None
        o_pair = []
        for i, (t, h) in enumerate(chains):
            c, hh = divmod(h, 2)
            sl = slice(c * V7X_LANES, (c + 1) * V7X_LANES)
            rows = slice(t * NA_Q_TILE, (t + 1) * NA_Q_TILE)
            if scores_of_step_j:
                qm = _mask_head(q_ref[0, rows, sl], hh)
                s_loc = _dot_nt(qm, k_ref[0, pl.ds(k0_a[t], NA_K_TILE), sl]) + bias_refs[t][0, 0, h]
                s_ctx = _dot_nt(qm, kc_ref[0, :, sl])
                m = jnp.maximum(s_loc.max(axis=-1, keepdims=True), s_ctx.max(axis=-1, keepdims=True))
                if pending is not None:
                    store_exp(*pending)
                pending = (i, s_loc, s_ctx, m)
            if values_of_step_before:
                vw = _with_ones(v_ref[0, pl.ds(k0_b[t], NA_K_TILE), sl], hh)
                vc = _with_ones(vc_ref[0, :, sl], hh)
                o_pair.append(_normalise(_dot(e_ref[i], vw) + _dot(ec_ref[i], vc)))
                if hh == 1:
                    o_ref[0, rows, sl] = jnp.where(low, o_pair[0], o_pair[1])
                    o_pair = []
        if pending is not None:
            store_exp(*pending)

    _first_middle_last(j, NA_STEPS, step)


def _na_block_picks():
    picks = np.full((3, NA_Q_ROWS, NA_K_ROWS), RPB_H, np.int32)
    for v, i in enumerate((0, 1, NA_BLOCKS - 1)):
        first_key_row = min(max(NA_Q_ROWS * i - NA_WIN_H // 2, 0), GRID_H - NA_K_ROWS)
        for a in range(NA_Q_ROWS):
            qr = NA_Q_ROWS * i + a
            rs = min(max(qr - NA_WIN_H // 2, 0), GRID_H - NA_WIN_H)
            for n in range(NA_K_ROWS):
                kr = first_key_row + n
                if rs <= kr < rs + NA_WIN_H:
                    picks[v, a, n] = kr - qr + NA_WIN_H - 1
    return picks


def _na_bias_kernel(ring_ref, o_ref, *, picks):
    ring = ring_ref[0, 0]
    qc = lax.broadcasted_iota(jnp.int32, (GRID_W, V7X_LANES), 0)
    kc = lax.broadcasted_iota(jnp.int32, (GRID_W, V7X_LANES), 1)
    cs = jnp.clip(qc - NA_WIN_W // 2, 0, GRID_W - NA_WIN_W)
    in_win = (kc >= cs) & (kc < cs + NA_WIN_W)
    blocks = []
    for r in range(RPB_H):
        rows = jnp.broadcast_to(ring[r:r + 1, :], (GRID_W, V7X_LANES))
        toe = pltpu.roll(rows, 0, 1, stride=1, stride_axis=0)
        blocks.append(jnp.where(in_win, toe * LOG2_E, NEG_INF)[:, :GRID_W])
    blocks.append(jnp.full((GRID_W, GRID_W), NEG_INF, F32))
    for v in range(picks.shape[0]):
        for a in range(NA_Q_ROWS):
            o_ref[0, v, 0, a * GRID_W:(a + 1) * GRID_W, :] = jnp.concatenate(
                [blocks[p] for p in picks[v, a]], axis=1)


def _na_bias_table(rpb):
    layers = rpb.shape[0]
    pad_rows = 16 - RPB_H
    ring = jnp.concatenate(
        [rpb[..., NA_WIN_W - 1:], jnp.zeros(rpb.shape[:3] + (V7X_LANES - RPB_W,), F32),
         rpb[..., :NA_WIN_W - 1]], axis=-1)
    ring = jnp.pad(ring, ((0, 0), (0, 0), (0, pad_rows), (0, 0)))
    return pl.pallas_call(
        functools.partial(_na_bias_kernel, picks=_na_block_picks()),
        grid=(layers, NA_HEADS),
        in_specs=[pl.BlockSpec((1, 1, RPB_H + pad_rows, V7X_LANES), lambda l, h: (l, h, 0, 0))],
        out_specs=pl.BlockSpec((1, 3, 1, NA_Q_TILE, NA_K_TILE), lambda l, h: (l, 0, h, 0, 0)),
        out_shape=jax.ShapeDtypeStruct((layers, 3, NA_HEADS, NA_Q_TILE, NA_K_TILE), F32),
        compiler_params=pltpu.CompilerParams(dimension_semantics=("parallel", "parallel")),
        name="na_bias_table",
    )(ring)


def _na_attention(qa, ka, va, bias, layer):
    step_a = lambda j: jnp.minimum(j, NA_STEPS - 1)
    step_b = lambda j: jnp.maximum(j - 1, 0)
    per_batch = NA_BLOCKS // NA_STEP_TILES
    chains = NA_STEP_TILES * NA_HEADS
    rows = NA_STEP_TILES * NA_Q_TILE

    def variant(t):
        def index(j):
            i = (step_a(j) * NA_STEP_TILES + t) % NA_BLOCKS
            return (layer, jnp.where(i == 0, 0, jnp.where(i == NA_BLOCKS - 1, 2, 1)), 0, 0, 0)
        return index

    blk = (rows * NA_WIDTH * (2 + 4) + 2 * (SEQ + CTX_LEN) * NA_WIDTH * 2
           + chains * NA_Q_TILE * NA_K_TILE * 4)
    scratch = chains * NA_Q_TILE * (NA_K_TILE + CTX_LEN) * 2
    scores = NA_HEADS * NA_Q_TILE * (NA_K_TILE + CTX_LEN) * 4
    bias_spec = lambda t: pl.BlockSpec((1, 1, NA_HEADS, NA_Q_TILE, NA_K_TILE), variant(t))
    return pl.pallas_call(
        _na_kernel,
        grid=(NA_STEPS + 1,),
        in_specs=[
            pl.BlockSpec((1, rows, NA_WIDTH), lambda j: (step_a(j) // per_batch, step_a(j) % per_batch, 0)),
            pl.BlockSpec((1, SEQ, NA_WIDTH), lambda j: (step_a(j) // per_batch, 0, 0)),
            pl.BlockSpec((1, CTX_LEN, NA_WIDTH), lambda j: (CTX_GROUP, step_a(j) // per_batch, 0)),
            bias_spec(0), bias_spec(1),
            pl.BlockSpec((1, SEQ, NA_WIDTH), lambda j: (step_b(j) // per_batch, 0, 0)),
            pl.BlockSpec((1, CTX_LEN, NA_WIDTH), lambda j: (CTX_GROUP, step_b(j) // per_batch, 0)),
        ],
        out_specs=pl.BlockSpec((1, rows, NA_WIDTH),
                               lambda j: (step_b(j) // per_batch, step_b(j) % per_batch, 0)),
        out_shape=jax.ShapeDtypeStruct((BATCH, SEQ, NA_WIDTH), F32),
        scratch_shapes=[pltpu.VMEM((chains, NA_Q_TILE, NA_K_TILE), BF16),
                        pltpu.VMEM((chains, NA_Q_TILE, CTX_LEN), BF16)],
        compiler_params=pltpu.CompilerParams(
            dimension_semantics=("arbitrary",),
            vmem_limit_bytes=_vmem_limit(blk, scratch, 2 * scores)),
        name="neighbourhood_attention",
    )(qa, ka, ka, bias, bias, va, va)


def _gqa_stack(q, kvh):
    return jnp.concatenate(
        [_mask_head(q[:, c * V7X_LANES:(c + 1) * V7X_LANES], kvh) for c in range(GQA_GROUP)], axis=0)


def _gqa_unstack(o_kv0, o_kv1, n_q):
    low = _low_lanes(n_q)
    return [jnp.where(low, o_kv0[c * n_q:(c + 1) * n_q], o_kv1[c * n_q:(c + 1) * n_q])
            for c in range(GQA_GROUP)]


def _gqa_kernel(q_ref, kt_ref, kct_ref, v_ref, vc_ref, o_ref, e_ref):
    j = pl.program_id(0)
    chains = [(t, kvh) for t in range(GQA_Q_TILE // GQA_SUB_TILE) for kvh in range(GQA_KV_HEADS)]
    n_lat = SEQ // GQA_KEY_CHUNK
    n_chunks = n_lat + CTX_LEN // GQA_KEY_CHUNK

    def chunk(ref_lat, ref_ctx, c, axis):
        src, cc = (ref_lat, c) if c < n_lat else (ref_ctx, c - n_lat)
        keys = slice(cc * GQA_KEY_CHUNK, (cc + 1) * GQA_KEY_CHUNK)
        return src[keys] if axis == 0 else src[0, :, keys]

    def step(scores_of_tile_j, values_of_tile_before):
        if values_of_tile_before:
            v_ones = [(_with_ones(v_ref[0], kvh), _with_ones(vc_ref[0], kvh)) for kvh in range(GQA_KV_HEADS)]
        scores, row_max, outs = {}, {}, {}
        for i in range(len(chains) + scores_of_tile_j):
            live = i < len(chains)
            if live:
                t, kvh = chains[i]
                rows = slice(t * GQA_SUB_TILE, (t + 1) * GQA_SUB_TILE)
            if live and scores_of_tile_j:
                stack = _gqa_stack(q_ref[0, rows, :], kvh)
                scores[i], lane_max = [], None
            acc = None
            for c in range(n_chunks):
                keys = slice(c * GQA_KEY_CHUNK, (c + 1) * GQA_KEY_CHUNK)
                if live and scores_of_tile_j:
                    s = _dot(stack, chunk(kt_ref, kct_ref, c, 1))
                    scores[i].append(s)
                    for w in range(GQA_KEY_CHUNK // V7X_LANES):
                        part = s[:, w * V7X_LANES:(w + 1) * V7X_LANES]
                        lane_max = part if lane_max is None else jnp.maximum(lane_max, part)
                if i >= 1 and scores_of_tile_j:
                    e_ref[i - 1, :, keys] = jnp.exp2(scores[i - 1][c] - row_max[i - 1]).astype(BF16)
                if live and values_of_tile_before:
                    part = _dot(e_ref[i, :, keys], chunk(v_ones[kvh][0], v_ones[kvh][1], c, 0))
                    acc = part if acc is None else acc + part
            if i >= 1 and scores_of_tile_j:
                del scores[i - 1]
            if live and scores_of_tile_j:
                row_max[i] = lane_max.max(axis=-1, keepdims=True)
            if live and values_of_tile_before:
                outs[kvh] = _normalise(acc)
                if kvh == GQA_KV_HEADS - 1:
                    for col, oc in enumerate(_gqa_unstack(outs[0], outs[1], GQA_SUB_TILE)):
                        o_ref[0, rows, col * V7X_LANES:(col + 1) * V7X_LANES] = oc

    _first_middle_last(j, GQA_TILES, step)


def _gqa_attention(qb, kbt, vb):
    tile_a = lambda j: jnp.minimum(j, GQA_TILES - 1)
    tile_b = lambda j: jnp.maximum(j - 1, 0)
    per_b = SEQ // GQA_Q_TILE
    chains = GQA_KV_HEADS * GQA_Q_TILE // GQA_SUB_TILE
    chain_rows = GQA_GROUP * GQA_SUB_TILE
    blk = (GQA_Q_TILE * GQA_WIDTH * (2 + 4) + 2 * (SEQ + CTX_LEN) * V7X_LANES * 2)
    scratch = chains * chain_rows * (SEQ + CTX_LEN) * 2
    scores = chain_rows * (SEQ + CTX_LEN) * 4
    return pl.pallas_call(
        _gqa_kernel,
        grid=(GQA_TILES + 1,),
        in_specs=[
            pl.BlockSpec((1, GQA_Q_TILE, GQA_WIDTH), lambda j: (tile_a(j) // per_b, tile_a(j) % per_b, 0)),
            pl.BlockSpec((1, GQA_KV_WIDTH, SEQ), lambda j: (tile_a(j) // per_b, 0, 0)),
            pl.BlockSpec((1, GQA_KV_WIDTH, CTX_LEN), lambda j: (CTX_GROUP, 0, tile_a(j) // per_b)),
            pl.BlockSpec((1, SEQ, GQA_KV_WIDTH), lambda j: (tile_b(j) // per_b, 0, 0)),
            pl.BlockSpec((1, CTX_LEN, GQA_KV_WIDTH), lambda j: (CTX_GROUP, tile_b(j) // per_b, 0)),
        ],
        out_specs=pl.BlockSpec((1, GQA_Q_TILE, GQA_WIDTH), lambda j: (tile_b(j) // per_b, tile_b(j) % per_b, 0)),
        out_shape=jax.ShapeDtypeStruct((BATCH, SEQ, GQA_WIDTH), F32),
        scratch_shapes=[pltpu.VMEM((chains, chain_rows, SEQ + CTX_LEN), BF16)],
        compiler_params=pltpu.CompilerParams(
            dimension_semantics=("arbitrary",),
            vmem_limit_bytes=_vmem_limit(blk, scratch, 4 * scores)),
        name="gqa_attention",
    )(qb, kbt, kbt, vb, vb)


def _ctx_kernel(qa_ref, ka_ref, va_ref, qb_ref, kbt_ref, vb_ref, oa_ref, ob_ref):
    low = _low_lanes(CTX_LEN)
    for b in range(CTX_STEP_BATCHES):
        rows = slice(b * CTX_LEN, (b + 1) * CTX_LEN)
        for c in range(NA_WIDTH // V7X_LANES):
            sl = slice(c * V7X_LANES, (c + 1) * V7X_LANES)
            q, k, v = qa_ref[0, rows, sl], ka_ref[0, rows, sl], va_ref[0, rows, sl]
            o = [_softmax_pv(_dot_nt(_mask_head(q, hh), k), v, hh) for hh in range(2)]
            oa_ref[0, rows, sl] = jnp.where(low, o[0], o[1])
        q = qb_ref[0, rows, :]
        o = [_softmax_pv(_dot(_gqa_stack(q, kvh), kbt_ref[0, :, rows]), vb_ref[0, rows, :], kvh)
             for kvh in range(GQA_KV_HEADS)]
        for c, oc in enumerate(_gqa_unstack(o[0], o[1], CTX_LEN)):
            ob_ref[0, rows, c * V7X_LANES:(c + 1) * V7X_LANES] = oc


def _ctx_attention(qa, ka, va, qb, kbt, vb):
    n_rows = CTX_STEP_BATCHES * CTX_LEN
    row = lambda b: (CTX_GROUP, b, 0)
    wide = pl.BlockSpec((1, n_rows, NA_WIDTH), row)
    narrow = pl.BlockSpec((1, n_rows, GQA_KV_WIDTH), row)
    narrow_t = pl.BlockSpec((1, GQA_KV_WIDTH, n_rows), lambda b: (CTX_GROUP, 0, b))
    out = pl.BlockSpec((1, n_rows, NA_WIDTH), lambda b: (0, b, 0))
    shape = jax.ShapeDtypeStruct((1, SEQ, NA_WIDTH), F32)
    blk = n_rows * (4 * NA_WIDTH * 2 + 2 * GQA_KV_WIDTH * 2 + 2 * NA_WIDTH * 4)
    return pl.pallas_call(
        _ctx_kernel,
        grid=(BATCH // CTX_STEP_BATCHES,),
        in_specs=[wide, wide, wide, wide, narrow_t, narrow],
        out_specs=[out, out],
        out_shape=[shape, shape],
        compiler_params=pltpu.CompilerParams(
            dimension_semantics=("parallel",),
            vmem_limit_bytes=_vmem_limit(blk, 0, 16 * GQA_GROUP * CTX_LEN * CTX_LEN * 4)),
        name="context_attention",
    )(qa, ka, va, qb, kbt, vb)


def _outproj_kernel(*refs, split_ctx):
    if split_ctx:
        x_ref, oa_ref, ob_ref, oac_ref, obc_ref, mod_ref, ga_ref, gb_ref, w_ref, o_ref = refs
        is_ctx = pl.program_id(0) == CTX_GROUP
        oa = jnp.where(is_ctx, oac_ref[0], oa_ref[0])
        ob = jnp.where(is_ctx, obc_ref[0], ob_ref[0])
    else:
        x_ref, oa_ref, ob_ref, mod_ref, ga_ref, gb_ref, w_ref, o_ref = refs
        oa, ob = oa_ref[0], ob_ref[0]
    gate = mod_ref[0][:, 2 * D_MODEL:]
    ya = (_rms(oa) * ga_ref[...]).astype(BF16)
    yb = (_rms(ob) * gb_ref[...]).astype(BF16)
    y = _dot(ya, w_ref[0, :NA_WIDTH, :]) + _dot(yb, w_ref[0, NA_WIDTH:, :])
    o_ref[0] = x_ref[0] + gate * y


def _out_projection(x, oa, ob, ctx_o, mod, ga, gb, w_out_b, layer):
    split_ctx = ctx_o is not None
    n_groups = N_GROUPS if split_ctx else BATCH
    row = lambda g, i: (g, i, 0)
    const2 = lambda g, i: (0, 0)
    last_tile = SEQ // ROW_TILE - 1
    lat = lambda g, i: (jnp.minimum(g, BATCH - 1), jnp.where(g == CTX_GROUP, last_tile, i), 0)
    cxt = lambda g, i: (0, jnp.where(g == CTX_GROUP, i, 0), 0)
    att = lambda imap: pl.BlockSpec((1, ROW_TILE, NA_WIDTH), imap)
    att_specs = [att(lat), att(lat), att(cxt), att(cxt)] if split_ctx else [att(row), att(row)]
    att_args = (oa, ob) + (tuple(ctx_o) if split_ctx else ())
    blk = ROW_TILE * (2 * D_MODEL + (2 + 2 * split_ctx) * NA_WIDTH) * 4
    return pl.pallas_call(
        functools.partial(_outproj_kernel, split_ctx=split_ctx),
        grid=(n_groups, SEQ // ROW_TILE),
        in_specs=[pl.BlockSpec((1, ROW_TILE, D_MODEL), row)] + att_specs + [
            pl.BlockSpec((1, 1, 3 * D_MODEL), lambda g, i: (g, 0, 1)),
            pl.BlockSpec((1, NA_WIDTH), const2),
            pl.BlockSpec((1, GQA_WIDTH), const2),
            pl.BlockSpec((1, NA_WIDTH + GQA_WIDTH, D_MODEL), lambda g, i: (layer, 0, 0),
                         pipeline_mode=pl.Buffered(1)),
        ],
        out_specs=pl.BlockSpec((1, ROW_TILE, D_MODEL), row),
        out_shape=jax.ShapeDtypeStruct((n_groups, SEQ, D_MODEL), F32),
        compiler_params=pltpu.CompilerParams(
            dimension_semantics=("arbitrary", "arbitrary"),
            vmem_limit_bytes=_vmem_limit(blk, (NA_WIDTH + GQA_WIDTH) * D_MODEL * 2, blk)),
        name="out_projection_ctx" if split_ctx else "out_projection",
    )(x, *att_args, mod, ga.reshape(1, NA_WIDTH), gb.reshape(1, GQA_WIDTH), w_out_b)


def _permute_gqa_heads(w, axis):
    shape = w.shape
    w = w.reshape(shape[:axis] + (GQA_KV_HEADS, GQA_GROUP, HEAD_DIM) + shape[axis + 1:])
    return jnp.swapaxes(w, axis, axis + 1).reshape(shape)


def _rope_tables():
    t = jnp.arange(SEQ, dtype=jnp.int32)
    row = (t // GRID_W).astype(F32)
    col = (t % GRID_W).astype(F32)
    axis_dim = HEAD_DIM // 2
    inv_freq = ROPE_THETA ** (-jnp.arange(0, axis_dim, 2, dtype=F32) / axis_dim)
    ang_r = row[:, None] * inv_freq[None, :]
    ang_c = col[:, None] * inv_freq[None, :]
    cos_r, sin_r, cos_c, sin_c = jnp.cos(ang_r), jnp.sin(ang_r), jnp.cos(ang_c), jnp.sin(ang_c)
    cos_h = jnp.concatenate([cos_r, cos_r, cos_c, cos_c], axis=-1)
    sin_h = jnp.concatenate([-sin_r, sin_r, -sin_c, sin_c], axis=-1)
    heads = V7X_LANES // HEAD_DIM
    cos2 = jnp.tile(cos_h, (1, heads))
    sin2 = jnp.tile(sin_h, (1, heads))
    return (jnp.stack([cos2, jnp.ones_like(cos2)]), jnp.stack([sin2, jnp.zeros_like(sin2)]))


def kernel(x, c, ctx, c_ctx, w_mod, b_mod, norm_ffn1, ffn1_w13, ffn1_w2, norm_mix, w_in, na_rpb,
           gqa_q_norm, gqa_k_norm, out_norm_a, out_norm_b, w_out, norm_ffn2, ffn2_w13, ffn2_w2, norm_f):
    cos_t, sin_t = _rope_tables()
    head_of_lane = np.arange(V7X_LANES) // HEAD_DIM
    ones_blk = np.asarray(head_of_lane[:, None] == head_of_lane[None, :], np.float32)
    ones2 = jnp.asarray(np.concatenate([ones_blk, ones_blk], axis=0), BF16)
    heads_per_col = V7X_LANES // HEAD_DIM
    bias_all = _na_bias_table(na_rpb)

    cond = jnp.zeros((COND_ROWS, D_MODEL), F32).at[:BATCH].set(c).at[CTX_GROUP].set(c_ctx)
    mod_all = _modulation(cond, w_mod, b_mod)

    w13b_1, w2b_1 = _cast_bf16(ffn1_w13), _cast_bf16(ffn1_w2)
    w13b_2, w2b_2 = _cast_bf16(ffn2_w13), _cast_bf16(ffn2_w2)
    qb0 = 3 * NA_WIDTH
    w_in_b = _cast_bf16(jnp.concatenate(
        [w_in[..., :qb0], _permute_gqa_heads(w_in[..., qb0:qb0 + GQA_WIDTH], 2), w_in[..., qb0 + GQA_WIDTH:]],
        axis=2))
    w_out_b = _cast_bf16(jnp.concatenate(
        [w_out[:, :NA_WIDTH], _permute_gqa_heads(w_out[:, NA_WIDTH:], 1)], axis=1))
    out_norm_b_cols = _permute_gqa_heads(out_norm_b, 1)

    xs = x
    for l in range(DEPTH):
        last = l == DEPTH - 1
        mod = mod_all[l].reshape(COND_ROWS, 1, N_MOD * D_MODEL)
        gq = jnp.tile(gqa_q_norm[l], heads_per_col).reshape(1, V7X_LANES)
        gk = jnp.tile(gqa_k_norm[l], heads_per_col).reshape(1, V7X_LANES)

        xs = _ffn_half(xs, mod, 0, norm_ffn1[l], w13b_1, w2b_1, l, N_GROUPS,
                       ctx=ctx.reshape(1, SEQ, D_MODEL) if l == 0 else None)
        qa, ka, va, qb, kbt, vb = _in_projection(xs, mod, norm_mix[l], w_in_b, l, ones2, gq, gk,
                                                 cos_t, sin_t)
        oa = _na_attention(qa, ka, va, bias_all, l)
        ob = _gqa_attention(qb, kbt, vb)
        ctx_o = None if last else _ctx_attention(qa, ka, va, qb, kbt, vb)
        xs = _out_projection(xs, oa, ob, ctx_o, mod, out_norm_a[l], out_norm_b_cols[l], w_out_b, l)
        xs = _ffn_half(xs, mod, 2, norm_ffn2[l], w13b_2, w2b_2, l, BATCH if last else N_GROUPS,
                       final_g=norm_f if last else None)
    return xs
```
